```python
import math
import jax
import jax.numpy as jnp
from jax import lax
import numpy as np

D_MODEL = 1024
BATCH = 2
SEQ = 16384
DEPTH = 2

EPS = 1e-6
NEG_INF = -1e30
D_FF = 2816
FFN_CONV = 3
RG_WIDTH = 512
RG_HEADS = 8
RG_HEAD_DIM = RG_WIDTH // RG_HEADS
RG_CONV = 4
RG_C = 8.0
S5_WIDTH = 256
S5_GROUP = 16
S5_GROUPS = S5_WIDTH // S5_GROUP
S5_STATE = 64
HG_HEADS = 4
HG_DK = 128
HG_DV = 128
HG_KW = HG_HEADS * HG_DK
HG_VW = HG_HEADS * HG_DV
HG_CHUNK = 64
DA_HEADS = 4
DA_HEAD_DIM = 64
DA_PATTERNS = ((128, 1), (512, 4), (2048, 16))
DA_GROUPS = len(DA_PATTERNS)
DA_BLOCK = 128
DA_WIDTH = DA_HEADS * DA_HEAD_DIM
EVEN_IN = 2 * RG_WIDTH + S5_WIDTH
EVEN_MIX = RG_WIDTH + S5_WIDTH
ODD_IN = 2 * HG_KW + 2 * HG_VW + DA_GROUPS * 3 * DA_WIDTH
ODD_MIX = HG_VW + DA_WIDTH
N_EVEN = (DEPTH + 1) // 2
N_ODD = DEPTH // 2

kernel_name = 'hybrid_rglru_s5_hgrn2_dilated_trunk'


def rmsnorm(x, g):
    x32 = x.astype(jnp.float32)
    y = x32 * lax.rsqrt(jnp.mean(x32 * x32, axis=-1, keepdims=True) + EPS)
    return (y * g.astype(jnp.float32)).astype(x.dtype)


def causal_depthwise_conv(x, w, b):
    k = w.shape[0]
    y = lax.conv_general_dilated(x, w[:, None, :].astype(x.dtype), window_strides=(1,),
                                 padding=[(k - 1, 0)], dimension_numbers=('NWC', 'WIO', 'NWC'),
                                 feature_group_count=x.shape[-1])
    return y + b.astype(x.dtype)


def _lin_combine(e1, e2):
    a1, b1 = e1
    a2, b2 = e2
    return a1 * a2, a2 * b1 + b2


def _complex_combine(e1, e2):
    ar1, ai1, br1, bi1 = e1
    ar2, ai2, br2, bi2 = e2
    return (ar2 * ar1 - ai2 * ai1, ar2 * ai1 + ai2 * ar1,
            ar2 * br1 - ai2 * bi1 + br2, ar2 * bi1 + ai2 * br1 + bi2)


def alibi_slopes(n):
    return 2.0 ** (-8.0 * jnp.arange(1, n + 1, dtype=jnp.float32) / n)


def rglru_mixer(xa, gate, conv_w, conv_b, w_a, b_a, w_x, b_x, lam):
    bsz, seq, _ = xa.shape
    u = causal_depthwise_conv(xa, conv_w, conv_b)
    uh = u.reshape(bsz, seq, RG_HEADS, RG_HEAD_DIM)
    r = jax.nn.sigmoid(jnp.einsum('bshi,hij->bshj', uh, w_a).reshape(bsz, seq, RG_WIDTH) + b_a)
    i = jax.nn.sigmoid(jnp.einsum('bshi,hij->bshj', uh, w_x).reshape(bsz, seq, RG_WIDTH) + b_x)
    log_a = (RG_C * r.astype(jnp.float32)) * jax.nn.log_sigmoid(lam.astype(jnp.float32))
    a = jnp.exp(log_a)
    bt = jnp.sqrt(-jnp.expm1(2.0 * log_a)) * (i * u).astype(jnp.float32)
    _, h = lax.associative_scan(_lin_combine, (a, bt), axis=1)
    return h.astype(xa.dtype) * jax.nn.gelu(gate)


def s5_mixer(u, a_re, a_im, b_re, b_im, c_re, c_im, d, log_dt, w_glu, b_glu):
    bsz, seq, _ = u.shape
    f32 = jnp.float32
    u32 = u.astype(f32)
    ug = u32.reshape(bsz, seq, S5_GROUPS, S5_GROUP)
    dt = jnp.exp(log_dt.astype(f32))[:, None]
    ar = a_re.astype(f32)
    ai = a_im.astype(f32)
    mag = jnp.exp(ar * dt)
    abar_re = mag * jnp.cos(ai * dt)
    abar_im = mag * jnp.sin(ai * dt)
    den = ar * ar + ai * ai
    num_re = abar_re - 1.0
    f_re = (num_re * ar + abar_im * ai) / den
    f_im = (abar_im * ar - num_re * ai) / den
    br = b_re.astype(f32)
    bi = b_im.astype(f32)
    bb_re = f_re[..., None] * br - f_im[..., None] * bi
    bb_im = f_re[..., None] * bi + f_im[..., None] * br
    bu_re = jnp.einsum('bsgp,gnp->bsgn', ug, bb_re)
    bu_im = jnp.einsum('bsgp,gnp->bsgn', ug, bb_im)
    a_t_re = jnp.broadcast_to(abar_re, bu_re.shape)
    a_t_im = jnp.broadcast_to(abar_im, bu_re.shape)
    _, _, h_re, h_im = lax.associative_scan(_complex_combine, (a_t_re, a_t_im, bu_re, bu_im), axis=1)
    y = (jnp.einsum('bsgn,gpn->bsgp', h_re, c_re.astype(f32))
         - jnp.einsum('bsgn,gpn->bsgp', h_im, c_im.astype(f32)))
    y = y.reshape(bsz, seq, S5_WIDTH) + d.astype(f32) * u32
    v = jax.nn.gelu(y)
    out = v * jax.nn.sigmoid(v @ w_glu.astype(f32) + b_glu.astype(f32))
    return out.astype(u.dtype)


def hgrn2_mixer(q, f_pre, i, g, lb, norm_w):
    bsz, seq, _ = q.shape
    f32 = jnp.float32
    nc = seq // HG_CHUNK

    def heads(t, dim):
        return t.reshape(bsz, nc, HG_CHUNK, HG_HEADS, dim).transpose(0, 3, 1, 2, 4)

    lb = lb.astype(f32)
    z = f_pre.astype(f32)
    log_f = jnp.log(lb + (1.0 - lb) * jax.nn.sigmoid(z))
    k = (1.0 - lb) * jax.nn.sigmoid(-z)
    qh = heads(jax.nn.silu(q.astype(f32)), HG_DK)
    kh = heads(k, HG_DK)
    lf = heads(log_f, HG_DK)
    vh = heads(i.astype(f32), HG_DV)
    cum = jnp.cumsum(lf, axis=3)
    mid = cum[:, :, :, HG_CHUNK // 2:HG_CHUNK // 2 + 1]
    last = cum[:, :, :, -1:]
    causal = jnp.tril(jnp.ones((HG_CHUNK, HG_CHUNK), dtype=bool))
    scores = jnp.einsum('bhncd,bhnsd->bhncs', qh * jnp.exp(cum - mid), kh * jnp.exp(mid - cum))
    o_intra = jnp.einsum('bhncs,bhnsv->bhncv', jnp.where(causal, scores, 0.0), vh)
    q_inter = (qh * jnp.exp(cum)).transpose(2, 0, 1, 3, 4)
    contrib = jnp.einsum('bhncd,bhncv->nbhdv', kh * jnp.exp(last - cum), vh)
    decay = jnp.exp(last[:, :, :, 0]).transpose(2, 0, 1, 3)

    def step(state, inp):
        dec, con, qi = inp
        out = jnp.einsum('bhcd,bhdv->bhcv', qi, state)
        return dec[..., None] * state + con, out

    init = jnp.zeros((bsz, HG_HEADS, HG_DK, HG_DV), f32)
    _, o_inter = lax.scan(step, init, (decay, contrib, q_inter))
    o = o_intra + o_inter.transpose(1, 2, 0, 3, 4)
    o = o.transpose(0, 2, 3, 1, 4).reshape(bsz, seq, HG_HEADS, HG_DV)
    o = o * lax.rsqrt(jnp.mean(o * o, axis=-1, keepdims=True) + EPS) * norm_w.astype(f32)
    gh = g.astype(f32).reshape(bsz, seq, HG_HEADS, HG_DV)
    return (o * jax.nn.silu(gh)).reshape(bsz, seq, HG_VW).astype(q.dtype)


def dilated_attention_group(q, k, v, window, dilation, slopes):
    bsz, seq, nh, hd = q.shape
    span = dilation * DA_BLOCK
    padded = -(-seq // span) * span
    sub = padded // dilation
    nb = sub // DA_BLOCK
    steps = window // dilation

    def blocks(t):
        t = jnp.pad(t, ((0, 0), (0, padded - seq), (0, 0), (0, 0)))
        t = t.reshape(bsz, sub, dilation, nh, hd).transpose(0, 2, 3, 1, 4)
        return t.reshape(bsz, dilation, nh, nb, DA_BLOCK, hd)

    def with_prev(t):
        prev = jnp.pad(t[:, :, :, :-1], ((0, 0), (0, 0), (0, 0), (1, 0), (0, 0), (0, 0)))
        return jnp.concatenate([prev, t], axis=4)

    qb = blocks(q)
    kk = with_prev(blocks(k))
    vv = with_prev(blocks(v))
    scores = jnp.einsum('brhnqd,brhnkd->brhnqk', qb, kk).astype(jnp.float32) * (hd ** -0.5)
    qi = jnp.arange(DA_BLOCK)[:, None]
    kj = jnp.arange(2 * DA_BLOCK)[None, :]
    rel = qi + DA_BLOCK - kj
    first = (jnp.arange(nb) == 0)[:, None, None] & (kj < DA_BLOCK)[None]
    valid = (rel >= 0) & (rel <= steps) & ~first
    bias = -(slopes.astype(jnp.float32) * dilation)[:, None, None, None] * rel.astype(jnp.float32)
    scores = jnp.where(valid, scores + bias, NEG_INF)
    lse = jax.nn.logsumexp(scores, axis=-1)
    probs = jnp.exp(scores - lse[..., None])
    out = jnp.einsum('brhnqk,brhnkd->brhnqd', probs.astype(v.dtype), vv)
    out = out.reshape(bsz, dilation, nh, sub, hd).transpose(0, 3, 1, 2, 4).reshape(bsz, padded, nh, hd)[:, :seq]
    lse = lse.reshape(bsz, dilation, nh, sub).transpose(0, 3, 1, 2).reshape(bsz, padded, nh)[:, :seq]
    return out, lse


def dilated_mixer(qkv):
    bsz, seq = qkv.shape[:2]
    slopes = alibi_slopes(DA_GROUPS * DA_HEADS).reshape(DA_GROUPS, DA_HEADS)
    outs = []
    lses = []
    for gi, (window, dilation) in enumerate(DA_PATTERNS):
        o, l = dilated_attention_group(qkv[:, :, gi, 0], qkv[:, :, gi, 1], qkv[:, :, gi, 2],
                                       window, dilation, slopes[gi])
        outs.append(o.astype(jnp.float32))
        lses.append(l)
    wts = jax.nn.softmax(jnp.stack(lses, axis=0), axis=0)
    out = jnp.einsum('gbsh,gbshd->bshd', wts, jnp.stack(outs, axis=0))
    return out.reshape(bsz, seq, DA_WIDTH).astype(qkv.dtype)


def conv_glu_ffn(h, w_up, conv_w, conv_b, w_down):
    up = causal_depthwise_conv(h @ w_up, conv_w, conv_b)
    val, gate = jnp.split(up, 2, axis=-1)
    return (jax.nn.gelu(gate) * val) @ w_down


def setup_inputs(seed: int = 0) -> dict:
    key = jax.random.key(seed)
    ks = iter(jax.random.split(key, 40))
    f32 = jnp.float32

    def nrm(shape, scale):
        return jax.random.normal(next(ks), shape, f32) * scale

    def unif(shape, lo, hi):
        return jax.random.uniform(next(ks), shape, f32, lo, hi)

    ne, no = N_EVEN, N_ODD
    a_mag = unif((ne, RG_WIDTH), 0.9, 0.999) ** (1.0 / RG_C)
    return {
        'x': nrm((BATCH, SEQ, D_MODEL), 1.0),
        'norm_g': 1.0 + nrm((DEPTH, 4, D_MODEL), 0.02),
        'ffn_w_up': nrm((DEPTH, D_MODEL, 2 * D_FF), D_MODEL ** -0.5),
        'ffn_conv_w': nrm((DEPTH, FFN_CONV, 2 * D_FF), FFN_CONV ** -0.5),
        'ffn_conv_b': nrm((DEPTH, 2 * D_FF), 0.01),
        'ffn_w_down': nrm((DEPTH, D_FF, D_MODEL), D_FF ** -0.5),
        'ev_w_in': nrm((ne, D_MODEL, EVEN_IN), D_MODEL ** -0.5),
        'ev_w_out': nrm((ne, EVEN_MIX, D_MODEL), EVEN_MIX ** -0.5),
        'rg_conv_w': nrm((ne, RG_CONV, RG_WIDTH), RG_CONV ** -0.5),
        'rg_conv_b': nrm((ne, RG_WIDTH), 0.01),
        'rg_w_a': nrm((ne, RG_HEADS, RG_HEAD_DIM, RG_HEAD_DIM), RG_HEAD_DIM ** -0.5),
        'rg_b_a': nrm((ne, RG_WIDTH), 0.01),
        'rg_w_x': nrm((ne, RG_HEADS, RG_HEAD_DIM, RG_HEAD_DIM), RG_HEAD_DIM ** -0.5),
        'rg_b_x': nrm((ne, RG_WIDTH), 0.01),
        'rg_lambda': jnp.log(a_mag) - jnp.log1p(-a_mag),
        's5_a_re': -0.5 + nrm((ne, S5_GROUPS, S5_STATE), 0.01),
        's5_a_im': math.pi * jnp.arange(S5_STATE, dtype=f32) + nrm((ne, S5_GROUPS, S5_STATE), 0.01),
        's5_b_re': nrm((ne, S5_GROUPS, S5_STATE, S5_GROUP), (2 * S5_GROUP) ** -0.5),
        's5_b_im': nrm((ne, S5_GROUPS, S5_STATE, S5_GROUP), (2 * S5_GROUP) ** -0.5),
        's5_c_re': nrm((ne, S5_GROUPS, S5_GROUP, S5_STATE), (2 * S5_STATE) ** -0.5),
        's5_c_im': nrm((ne, S5_GROUPS, S5_GROUP, S5_STATE), (2 * S5_STATE) ** -0.5),
        's5_d': nrm((ne, S5_WIDTH), 0.5),
        's5_log_dt': unif((ne, S5_GROUPS), math.log(1e-3), math.log(1e-1)),
        's5_w_glu': nrm((ne, S5_WIDTH, S5_WIDTH), S5_WIDTH ** -0.5),
        's5_b_glu': nrm((ne, S5_WIDTH), 0.01),
        'od_w_in': nrm((no, D_MODEL, ODD_IN), D_MODEL ** -0.5),
        'od_w_out': nrm((no, ODD_MIX, D_MODEL), ODD_MIX ** -0.5),
        'hg_lower': nrm((DEPTH, HG_KW), 0.1),
        'hg_norm_g': 1.0 + nrm((no, HG_DV), 0.02),
    }


def reference(x, norm_g, ffn_w_up, ffn_conv_w, ffn_conv_b, ffn_w_down, ev_w_in, ev_w_out,
              rg_conv_w, rg_conv_b, rg_w_a, rg_b_a, rg_w_x, rg_b_x, rg_lambda,
              s5_a_re, s5_a_im, s5_b_re, s5_b_im, s5_c_re, s5_c_im, s5_d, s5_log_dt, s5_w_glu, s5_b_glu,
              od_w_in, od_w_out, hg_lower, hg_norm_g):
    lb_p = jax.nn.softmax(hg_lower.astype(jnp.float32), axis=0)
    lb_all = jnp.cumsum(lb_p, axis=0) - lb_p[0]
    h = x
    for layer in range(DEPTH):
        j = layer // 2
        g = norm_g[layer]
        y = rmsnorm(h, g[0])
        if layer % 2 == 0:
            proj = y @ ev_w_in[j]
            ya = rglru_mixer(proj[..., :RG_WIDTH], proj[..., RG_WIDTH:2 * RG_WIDTH],
                             rg_conv_w[j], rg_conv_b[j], rg_w_a[j], rg_b_a[j], rg_w_x[j], rg_b_x[j], rg_lambda[j])
            yb = s5_mixer(proj[..., 2 * RG_WIDTH:], s5_a_re[j], s5_a_im[j], s5_b_re[j], s5_b_im[j],
                          s5_c_re[j], s5_c_im[j], s5_d[j], s5_log_dt[j], s5_w_glu[j], s5_b_glu[j])
            y = jnp.concatenate([ya, yb], axis=-1) @ ev_w_out[j]
        else:
            bsz, seq = y.shape[:2]
            proj = y @ od_w_in[j]
            o0 = 2 * HG_KW
            o1 = o0 + HG_VW
            o2 = o1 + HG_VW
            yc = hgrn2_mixer(proj[..., :HG_KW], proj[..., HG_KW:o0], proj[..., o0:o1], proj[..., o1:o2],
                             lb_all[layer], hg_norm_g[j])
            qkv = proj[..., o2:].reshape(bsz, seq, DA_GROUPS, 3, DA_HEADS, DA_HEAD_DIM)
            yd = dilated_mixer(qkv)
            y = jnp.concatenate([yc, yd], axis=-1) @ od_w_out[j]
        h = h + rmsnorm(y, g[1])
        y = conv_glu_ffn(rmsnorm(h, g[2]), ffn_w_up[layer], ffn_conv_w[layer], ffn_conv_b[layer], ffn_w_down[layer])
        h = h + rmsnorm(y, g[3])
    return h
```

```python
import functools
import math

import jax
import jax.numpy as jnp
from jax import lax
from jax.experimental import pallas as pl
from jax.experimental.pallas import tpu as pltpu

f32 = jnp.float32
bf16 = jnp.bfloat16

EPS = 1e-6
NEG_INF = -1e30
RG_C = 8.0
RG_HEADS = 8
RG_WIDTH = 512
S5_WIDTH = 256
S5_GROUP = 16
S5_STATE = 64
HG_HEADS = 4
HG_DK = 128
HG_CHUNK = 64
HG_W = HG_HEADS * HG_DK
DA_HEADS = 4
DA_HEAD_DIM = 64
DA_WIDTH = DA_HEADS * DA_HEAD_DIM
DA_PATTERNS = ((128, 1), (512, 4), (2048, 16))
DA_BLOCK = 128
DA_SPAN = 2048
LANES = 128
SUBLANES = 8
VMEM_LIMIT = 56 * 1024 * 1024


def _rms(x, g):
    return x * lax.rsqrt(jnp.mean(x * x, axis=-1, keepdims=True) + EPS) * g


def _dot(a, b):
    return jnp.dot(a, b, preferred_element_type=f32)


def _dot_nt(a, b):
    return lax.dot_general(a, b, (((1,), (1,)), ((), ())), preferred_element_type=f32)


def _dot_tn(a, b):
    return lax.dot_general(a, b, (((0,), (0,)), ((), ())), preferred_element_type=f32)


def _const_spec(shape):
    nd = len(shape)
    return pl.BlockSpec(shape, lambda *_: (0,) * nd, pipeline_mode=pl.Buffered(1))


def _params(sem):
    return pltpu.CompilerParams(dimension_semantics=sem, vmem_limit_bytes=VMEM_LIMIT)


PROJ_TM = 256
PROJ_CHUNK = 512


def _norm_proj_kernel(x_ref, g_ref, w_ref, *out_refs, layout):
    xn = _rms(x_ref[...], g_ref[...]).astype(bf16)
    col = 0
    for o_ref, (kind, width) in zip(out_refs, layout):
        for c0 in range(0, width, PROJ_CHUNK):
            cw = min(PROJ_CHUNK, width - c0)
            res = _dot(xn, w_ref[:, col + c0:col + c0 + cw])
            if kind == "flat":
                o_ref[:, c0:c0 + cw] = res
            else:
                for s in range(cw // LANES):
                    o_ref[c0 // LANES + s] = res[:, s * LANES:(s + 1) * LANES]
        col += width


def _norm_proj(h2d, g, w, layout):
    n, d = h2d.shape
    tm = PROJ_TM
    out_shape, out_specs = [], []
    for kind, width in layout:
        if kind == "flat":
            out_shape.append(jax.ShapeDtypeStruct((n, width), f32))
            out_specs.append(pl.BlockSpec((tm, width), lambda i: (i, 0)))
        else:
            out_shape.append(jax.ShapeDtypeStruct((width // LANES, n, LANES), f32))
            out_specs.append(pl.BlockSpec((width // LANES, tm, LANES), lambda i: (0, i, 0)))
    return pl.pallas_call(
        functools.partial(_norm_proj_kernel, layout=layout),
        grid=(n // tm,),
        in_specs=[pl.BlockSpec((tm, d), lambda i: (i, 0)),
                  _const_spec((1, d)),
                  _const_spec(w.shape)],
        out_specs=out_specs,
        out_shape=out_shape,
        compiler_params=_params(("arbitrary",)),
        name="norm_proj",
    )(h2d, g.reshape(1, d), w)


EV_T = 256


def _even_kernel(p_ref, rcw_ref, rcb_ref, wa_ref, ba_ref, wx_ref, bx_ref, lam_ref,
                 bre_ref, bim_ref, lp_ref, pre_ref, pim_ref, cre_ref, cim_ref,
                 d5_ref, wg_ref, bg_ref,
                 ya_ref, yb_ref,
                 xst, rga, rgb, rgc, sre, sim, scr, sci):
    T = EV_T
    t = pl.program_id(1)

    @pl.when(t == 0)
    def _():
        xst[0:SUBLANES, :] = jnp.zeros((SUBLANES, RG_WIDTH), f32)
        rgc[...] = jnp.zeros_like(rgc)
        scr[...] = jnp.zeros_like(scr)
        sci[...] = jnp.zeros_like(sci)

    xst[SUBLANES:SUBLANES + T, :] = p_ref[:, 0:RG_WIDTH]
    kc = rcw_ref.shape[0]
    u = rcb_ref[...]
    for k in range(kc):
        off = SUBLANES - (kc - 1) + k
        u = u + rcw_ref[k:k + 1, :] * xst[off:off + T, :]
    xst[0:SUBLANES, :] = xst[T:T + SUBLANES, :]
    ub = u.astype(bf16)
    r = jax.nn.sigmoid(_dot(ub, wa_ref[...]) + ba_ref[...])
    ig = jax.nn.sigmoid(_dot(ub, wx_ref[...]) + bx_ref[...])
    log_a = (RG_C * r) * jax.nn.log_sigmoid(lam_ref[...])
    a = jnp.exp(log_a)
    bt = jnp.sqrt(-jnp.tanh(log_a) * (a * a + 1.0)) * (ig * u)
    rowm = lax.broadcasted_iota(jnp.int32, (T, RG_WIDTH), 0) & (SUBLANES - 1)
    for s in (1, 2, 4):
        m = rowm >= s
        bt = bt + jnp.where(m, a * pltpu.roll(bt, s, 0), 0.0)
        a = jnp.where(m, a * pltpu.roll(a, s, 0), a)
    rga[...] = a
    rgb[...] = bt

    def rg_body(gidx, c):
        r0 = pl.multiple_of(gidx * SUBLANES, SUBLANES)
        h = rga[pl.ds(r0, SUBLANES), :] * c + rgb[pl.ds(r0, SUBLANES), :]
        rgb[pl.ds(r0, SUBLANES), :] = h
        return jnp.broadcast_to(h[SUBLANES - 1:SUBLANES, :], (SUBLANES, RG_WIDTH))

    rgc[...] = lax.fori_loop(0, T // SUBLANES, rg_body, rgc[...], unroll=4)
    ya_ref[...] = rgb[...] * jax.nn.gelu(p_ref[:, RG_WIDTH:2 * RG_WIDTH])

    u5 = p_ref[:, 2 * RG_WIDTH:2 * RG_WIDTH + S5_WIDTH]
    u5b = u5.astype(bf16)
    hre = _dot(u5b, bre_ref[...])
    him = _dot(u5b, bim_ref[...])
    ns = hre.shape[1]
    rowm5 = lax.broadcasted_iota(jnp.int32, (T, ns), 0) & (SUBLANES - 1)
    for i, s in enumerate((1, 2, 4)):
        m = rowm5 >= s
        cr = lp_ref[2 * i:2 * i + 1, :]
        ci = lp_ref[2 * i + 1:2 * i + 2, :]
        zr = jnp.where(m, pltpu.roll(hre, s, 0), 0.0)
        zi = jnp.where(m, pltpu.roll(him, s, 0), 0.0)
        hre, him = hre + (cr * zr - ci * zi), him + (cr * zi + ci * zr)
    sre[...] = hre
    sim[...] = him

    def s5_body(gidx, c):
        cre, cim = c
        r0 = pl.multiple_of(gidx * SUBLANES, SUBLANES)
        pr = pre_ref[...]
        pi = pim_ref[...]
        hr = sre[pl.ds(r0, SUBLANES), :] + (pr * cre - pi * cim)
        hi = sim[pl.ds(r0, SUBLANES), :] + (pr * cim + pi * cre)
        sre[pl.ds(r0, SUBLANES), :] = hr
        sim[pl.ds(r0, SUBLANES), :] = hi
        return (jnp.broadcast_to(hr[SUBLANES - 1:SUBLANES, :], (SUBLANES, ns)),
                jnp.broadcast_to(hi[SUBLANES - 1:SUBLANES, :], (SUBLANES, ns)))

    cre, cim = lax.fori_loop(0, T // SUBLANES, s5_body, (scr[...], sci[...]), unroll=2)
    scr[...] = cre
    sci[...] = cim
    y5 = (_dot(sre[...].astype(bf16), cre_ref[...]) - _dot(sim[...].astype(bf16), cim_ref[...])
          + d5_ref[...] * u5)
    v = jax.nn.gelu(y5)
    yb_ref[...] = v * jax.nn.sigmoid(_dot(v.astype(bf16), wg_ref[...]) + bg_ref[...])


def _block_diag(w):
    g, a, b = w.shape
    eye = jnp.eye(g, dtype=w.dtype)
    return (eye[:, None, :, None] * w[:, :, None, :]).reshape(g * a, g * b)


def _even_mixer(proj, bsz, seq, rg_conv_w, rg_conv_b, rg_w_a, rg_b_a, rg_w_x, rg_b_x, rg_lambda,
                s5_a_re, s5_a_im, s5_b_re, s5_b_im, s5_c_re, s5_c_im, s5_d, s5_log_dt,
                s5_w_glu, s5_b_glu):
    n = proj.shape[0]
    T = EV_T
    nt = seq // T
    ns = s5_a_re.shape[0] * S5_STATE
    dt = jnp.exp(s5_log_dt.astype(f32))[:, None]
    ar = s5_a_re.astype(f32)
    ai = s5_a_im.astype(f32)
    mag = jnp.exp(ar * dt)
    abar_re = mag * jnp.cos(ai * dt)
    abar_im = mag * jnp.sin(ai * dt)
    den = ar * ar + ai * ai
    num_re = abar_re - 1.0
    f_re = (num_re * ar + abar_im * ai) / den
    f_im = (abar_im * ar - num_re * ai) / den
    br = s5_b_re.astype(f32)
    bi = s5_b_im.astype(f32)
    bb_re = f_re[..., None] * br - f_im[..., None] * bi
    bb_im = f_re[..., None] * bi + f_im[..., None] * br
    bre_bd = _block_diag(jnp.swapaxes(bb_re, 1, 2)).astype(bf16)
    bim_bd = _block_diag(jnp.swapaxes(bb_im, 1, 2)).astype(bf16)
    cre_bd = _block_diag(jnp.swapaxes(s5_c_re.astype(f32), 1, 2)).astype(bf16)
    cim_bd = _block_diag(jnp.swapaxes(s5_c_im.astype(f32), 1, 2)).astype(bf16)
    l1r = abar_re.reshape(1, ns)
    l1i = abar_im.reshape(1, ns)
    pows = [(l1r, l1i)]
    for _ in range(SUBLANES - 1):
        pr, pi = pows[-1]
        pows.append((pr * l1r - pi * l1i, pr * l1i + pi * l1r))
    lp = jnp.concatenate([pows[0][0], pows[0][1], pows[1][0], pows[1][1], pows[3][0], pows[3][1],
                          jnp.zeros((2, ns), f32)], axis=0)
    p_re = jnp.concatenate([p[0] for p in pows], axis=0)
    p_im = jnp.concatenate([p[1] for p in pows], axis=0)
    wa_bd = _block_diag(rg_w_a).astype(bf16)
    wx_bd = _block_diag(rg_w_x).astype(bf16)
    row = lambda v: v.reshape(1, -1).astype(f32)
    consts = [rg_conv_w.astype(f32), row(rg_conv_b), wa_bd, row(rg_b_a), wx_bd, row(rg_b_x), row(rg_lambda),
              bre_bd, bim_bd, lp, p_re, p_im, cre_bd, cim_bd, row(s5_d),
              s5_w_glu.astype(bf16), row(s5_b_glu)]
    width = proj.shape[1]
    return pl.pallas_call(
        _even_kernel,
        grid=(bsz, nt),
        in_specs=[pl.BlockSpec((T, width), lambda b, t: (b * nt + t, 0))]
                 + [_const_spec(c.shape) for c in consts],
        out_specs=[pl.BlockSpec((T, RG_WIDTH), lambda b, t: (b * nt + t, 0)),
                   pl.BlockSpec((T, S5_WIDTH), lambda b, t: (b * nt + t, 0))],
        out_shape=[jax.ShapeDtypeStruct((n, RG_WIDTH), f32),
                   jax.ShapeDtypeStruct((n, S5_WIDTH), f32)],
        scratch_shapes=[pltpu.VMEM((T + SUBLANES, RG_WIDTH), f32),
                        pltpu.VMEM((T, RG_WIDTH), f32), pltpu.VMEM((T, RG_WIDTH), f32),
                        pltpu.VMEM((SUBLANES, RG_WIDTH), f32),
                        pltpu.VMEM((T, ns), f32), pltpu.VMEM((T, ns), f32),
                        pltpu.VMEM((SUBLANES, ns), f32), pltpu.VMEM((SUBLANES, ns), f32)],
        compiler_params=_params(("arbitrary", "arbitrary")),
        name="even_mixer",
    )(proj, *consts)


HG_T = 256


def _split3(x):
    hi = x.astype(bf16)
    r1 = x - hi.astype(f32)
    mid = r1.astype(bf16)
    lo = (r1 - mid.astype(f32)).astype(bf16)
    return hi, mid, lo


def _hgrn_kernel(p_ref, lb_ref, nw_ref, o_ref, st_ref):
    C = HG_CHUNK
    t = pl.program_id(1)

    @pl.when(t == 0)
    def _():
        st_ref[...] = jnp.zeros_like(st_ref)

    ri = lax.broadcasted_iota(jnp.int32, (C, C), 0)
    ci = lax.broadcasted_iota(jnp.int32, (C, C), 1)
    causal = ri >= ci
    tril = causal.astype(bf16)
    lb = lb_ref[...]

    def chunk(c, carry):
        r0 = pl.multiple_of(c * C, C)
        q = p_ref[pl.ds(r0, C), 0:HG_W]
        z = p_ref[pl.ds(r0, C), HG_W:2 * HG_W]
        log_f = jnp.log(lb + (1.0 - lb) * jax.nn.sigmoid(z))
        kk = (1.0 - lb) * jax.nn.sigmoid(-z)
        hi, mid_, lo = _split3(log_f)
        cum = _dot(tril, hi) + _dot(tril, mid_) + _dot(tril, lo)
        qs = jax.nn.silu(q)
        for h in range(HG_HEADS):
            sl = slice(h * HG_DK, (h + 1) * HG_DK)
            cum_h = cum[:, sl]
            midv = cum_h[C // 2:C // 2 + 1, :]
            last = cum_h[C - 1:C, :]
            q_h = qs[:, sl]
            k_h = kk[:, sl]
            v_h = p_ref[pl.ds(r0, C), 2 * HG_W + h * HG_DK:2 * HG_W + (h + 1) * HG_DK].astype(bf16)
            g_h = p_ref[pl.ds(r0, C), 3 * HG_W + h * HG_DK:3 * HG_W + (h + 1) * HG_DK]
            sc = _dot_nt((q_h * jnp.exp(cum_h - midv)).astype(bf16),
                         (k_h * jnp.exp(midv - cum_h)).astype(bf16))
            sc = jnp.where(causal, sc, 0.0)
            o = _dot(sc.astype(bf16), v_h)
            st = st_ref[h]
            o = o + _dot_nt((q_h * jnp.exp(cum_h)).astype(bf16), st.astype(bf16))
            con_t = _dot_tn(v_h, (k_h * jnp.exp(last - cum_h)).astype(bf16))
            st_ref[h] = jnp.exp(last) * st + con_t
            on = o * lax.rsqrt(jnp.mean(o * o, axis=-1, keepdims=True) + EPS) * nw_ref[...]
            o_ref[pl.ds(r0, C), sl] = on * jax.nn.silu(g_h)
        return carry

    lax.fori_loop(0, HG_T // C, chunk, 0)


def _hgrn_mixer(hg, bsz, seq, lb, norm_w):
    n = hg.shape[0]
    T = HG_T
    nt = seq // T
    return pl.pallas_call(
        _hgrn_kernel,
        grid=(bsz, nt),
        in_specs=[pl.BlockSpec((T, 4 * HG_W), lambda b, t: (b * nt + t, 0)),
                  _const_spec((1, HG_W)), _const_spec((1, HG_DK))],
        out_specs=pl.BlockSpec((T, HG_W), lambda b, t: (b * nt + t, 0)),
        out_shape=jax.ShapeDtypeStruct((n, HG_W), f32),
        scratch_shapes=[pltpu.VMEM((HG_HEADS, HG_DK, HG_DK), f32)],
        compiler_params=_params(("arbitrary", "arbitrary")),
        name="hgrn_mixer",
    )(hg, lb.reshape(1, HG_W).astype(f32), norm_w.reshape(1, HG_DK).astype(f32))


N_PAIR = DA_WIDTH // LANES
DA_MERGE_ROWS = 256


def _attn_group(gi, d, n, sd_ref, x_ref, qd, kv, og, lg, bias_scr):
    B = DA_BLOCK
    L = DA_SPAN // d
    nblk = L // B
    nblk_log = int(math.log2(nblk))
    stride_r = L + B

    @pl.when(n == 0)
    def _():
        for r in range(d):
            kv[:, r * stride_r:r * stride_r + B, :] = jnp.zeros((2 * N_PAIR, B, LANES), bf16)

    scale = DA_HEAD_DIM ** -0.5
    for r in range(d):
        rows = pl.ds(r, L, stride=d) if d > 1 else pl.ds(0, L)
        for s in range(N_PAIR):
            qd[s, r * L:(r + 1) * L, :] = (x_ref[s, rows, :] * scale).astype(bf16)
        for s in range(2 * N_PAIR):
            kv[s, r * stride_r + B:r * stride_r + B + L, :] = x_ref[N_PAIR + s, rows, :].astype(bf16)

    qi = lax.broadcasted_iota(jnp.int32, (B, 2 * B), 0)
    kj = lax.broadcasted_iota(jnp.int32, (B, 2 * B), 1)
    rel = qi + B - kj
    valid = (rel >= 0) & (rel <= B)
    relf = rel.astype(f32)
    for h in range(DA_HEADS):
        bias = jnp.where(valid, -sd_ref[gi, h] * relf, NEG_INF)
        bias_scr[0, h] = bias
        bias_scr[1, h] = jnp.where(kj < B, NEG_INF, bias)

    lane = lax.broadcasted_iota(jnp.int32, (B, LANES), 1)
    lo_half = lane < DA_HEAD_DIM

    def blk_body(blk, carry):
        r = blk >> nblk_log
        m = blk & (nblk - 1)
        q0 = pl.multiple_of(blk * B, B)
        k0 = pl.multiple_of((blk + r) * B, B)
        first = jnp.logical_and(n == 0, m == 0).astype(jnp.int32)
        start = m * (B * d) + r
        orow = pl.ds(start, B, stride=d) if d > 1 else pl.ds(pl.multiple_of(start, B), B)
        for j in range(N_PAIR):
            qp = qd[j, pl.ds(q0, B), :]
            kp = kv[j, pl.ds(k0, 2 * B), :]
            vp = kv[N_PAIR + j, pl.ds(k0, 2 * B), :]
            outs, lses = [], []
            for e in range(2):
                keep = lo_half if e == 0 else jnp.logical_not(lo_half)
                qm = jnp.where(keep, qp, jnp.zeros_like(qp))
                s = _dot_nt(qm, kp) + bias_scr[first, 2 * j + e]
                mx = jnp.max(s, axis=-1, keepdims=True)
                ex = jnp.exp(s - mx)
                l = jnp.sum(ex, axis=-1, keepdims=True)
                outs.append(_dot(ex.astype(bf16), vp) * (1.0 / l))
                lses.append(jnp.broadcast_to(mx + jnp.log(l), (B, LANES)))
            og[gi, j, orow, :] = jnp.where(lo_half, outs[0], outs[1])
            lg[gi, j, orow, :] = jnp.where(lo_half, lses[0], lses[1])
        return carry

    lax.fori_loop(0, DA_SPAN // B, blk_body, 0)

    for r in range(d):
        kv[:, r * stride_r:r * stride_r + B, :] = kv[:, r * stride_r + L:r * stride_r + L + B, :]


def _attn_kernel(sd_ref, x_ref, o_ref, qd, kv0, kv1, kv2, og, lg, bias_scr):
    n = pl.program_id(1)
    g = pl.program_id(2)
    kvs = (kv0, kv1, kv2)
    for gi, (_, d) in enumerate(DA_PATTERNS):
        @pl.when(g == gi)
        def _(gi=gi, d=d):
            _attn_group(gi, d, n, sd_ref, x_ref, qd, kvs[gi], og, lg, bias_scr)

    @pl.when(g == len(DA_PATTERNS) - 1)
    def _():
        R = DA_MERGE_ROWS

        def merge(i, carry):
            r0 = pl.multiple_of(i * R, R)
            for j in range(N_PAIR):
                ls = [lg[gi, j, pl.ds(r0, R), :] for gi in range(len(DA_PATTERNS))]
                mx = functools.reduce(jnp.maximum, ls)
                ws = [jnp.exp(l - mx) for l in ls]
                den = functools.reduce(lambda a, b: a + b, ws)
                num = functools.reduce(lambda a, b: a + b,
                                       [w * og[gi, j, pl.ds(r0, R), :] for gi, w in enumerate(ws)])
                o_ref[pl.ds(r0, R), j * LANES:(j + 1) * LANES] = num * (1.0 / den)
            return carry

        lax.fori_loop(0, DA_SPAN // R, merge, 0)


def _attn_mixer(qkv_slabs, bsz, seq, sd):
    ng = len(DA_PATTERNS)
    n = qkv_slabs.shape[1]
    nt = seq // DA_SPAN
    per_g = 3 * N_PAIR
    kv_scratch = [pltpu.VMEM((2 * N_PAIR, DA_SPAN + DA_BLOCK * d, LANES), bf16) for _, d in DA_PATTERNS]
    return pl.pallas_call(
        _attn_kernel,
        grid=(bsz, nt, ng),
        in_specs=[pl.BlockSpec(memory_space=pltpu.SMEM),
                  pl.BlockSpec((per_g, DA_SPAN, LANES), lambda b, t, g: (g, b * nt + t, 0))],
        out_specs=pl.BlockSpec((DA_SPAN, DA_WIDTH), lambda b, t, g: (b * nt + t, 0)),
        out_shape=jax.ShapeDtypeStruct((n, DA_WIDTH), f32),
        scratch_shapes=[pltpu.VMEM((N_PAIR, DA_SPAN, LANES), bf16)] + kv_scratch + [
            pltpu.VMEM((ng, N_PAIR, DA_SPAN, LANES), f32),
            pltpu.VMEM((ng, N_PAIR, DA_SPAN, LANES), f32),
            pltpu.VMEM((2, DA_HEADS, DA_BLOCK, 2 * DA_BLOCK), f32)],
        compiler_params=_params(("arbitrary", "arbitrary", "arbitrary")),
        name="dilated_attn",
    )(sd, qkv_slabs)


POST_TM = 256
FF_CHUNK = 256


def _post_kernel(ya_ref, yb_ref, h_ref, woa_ref, wob_ref, g_ref, wup_ref, cw_ref, cb_ref, wdn_ref,
                 o_ref, xn_scr, st_scr, carry_scr, acc_scr):
    tm = POST_TM
    C = FF_CHUNK
    dff = wdn_ref.shape[0]
    kc = cw_ref.shape[0]
    t = pl.program_id(1)

    @pl.when(t == 0)
    def _():
        carry_scr[...] = jnp.zeros_like(carry_scr)

    y1 = _dot(ya_ref[...].astype(bf16), woa_ref[...]) + _dot(yb_ref[...].astype(bf16), wob_ref[...])
    h1 = h_ref[...] + _rms(y1, g_ref[1:2, :])
    o_ref[...] = h1
    xn_scr[...] = _rms(h1, g_ref[2:3, :]).astype(bf16)
    acc_scr[...] = jnp.zeros_like(acc_scr)
    for j in range(dff // C):
        slot = j % 2
        xn = xn_scr[...]
        c0 = j * C
        st_scr[slot, 0:SUBLANES, :] = carry_scr[j]
        st_scr[slot, SUBLANES:SUBLANES + tm, 0:C] = _dot(xn, wup_ref[:, c0:c0 + C])
        st_scr[slot, SUBLANES:SUBLANES + tm, C:2 * C] = _dot(xn, wup_ref[:, dff + c0:dff + c0 + C])
        cv = cb_ref[:, c0:c0 + C]
        cg = cb_ref[:, dff + c0:dff + c0 + C]
        for k in range(kc):
            off = SUBLANES - (kc - 1) + k
            cv = cv + cw_ref[k:k + 1, c0:c0 + C] * st_scr[slot, off:off + tm, 0:C]
            cg = cg + cw_ref[k:k + 1, dff + c0:dff + c0 + C] * st_scr[slot, off:off + tm, C:2 * C]
        carry_scr[j] = st_scr[slot, tm:tm + SUBLANES, :]
        act = (jax.nn.gelu(cg) * cv).astype(bf16)
        acc_scr[...] += _dot(act, wdn_ref[c0:c0 + C, :])
    o_ref[...] = o_ref[...] + _rms(acc_scr[...], g_ref[3:4, :])


def _post(ya, yb, h2d, bsz, seq, w_out, g, w_up, conv_w, conv_b, w_down):
    n, d = h2d.shape
    tm = POST_TM
    nt = seq // tm
    wa = ya.shape[1]
    wb = yb.shape[1]
    dff = w_down.shape[0]
    woa = w_out[:wa].astype(bf16)
    wob = w_out[wa:].astype(bf16)
    consts = [woa, wob, g.astype(f32), w_up.astype(bf16), conv_w.astype(f32),
              conv_b.reshape(1, -1).astype(f32), w_down.astype(bf16)]
    row_spec = lambda w: pl.BlockSpec((tm, w), lambda b, t: (b * nt + t, 0))
    return pl.pallas_call(
        _post_kernel,
        grid=(bsz, nt),
        in_specs=[row_spec(wa), row_spec(wb), row_spec(d)] + [_const_spec(c.shape) for c in consts],
        out_specs=row_spec(d),
        out_shape=jax.ShapeDtypeStruct((n, d), f32),
        scratch_shapes=[pltpu.VMEM((tm, d), bf16),
                        pltpu.VMEM((2, tm + SUBLANES, 2 * FF_CHUNK), f32),
                        pltpu.VMEM((dff // FF_CHUNK, SUBLANES, 2 * FF_CHUNK), f32),
                        pltpu.VMEM((tm, d), f32)],
        compiler_params=_params(("arbitrary", "arbitrary")),
        name="post_ffn",
    )(ya, yb, h2d, *consts)


def kernel(x, norm_g, ffn_w_up, ffn_conv_w, ffn_conv_b, ffn_w_down, ev_w_in, ev_w_out, rg_conv_w, rg_conv_b,
           rg_w_a, rg_b_a, rg_w_x, rg_b_x, rg_lambda, s5_a_re, s5_a_im, s5_b_re, s5_b_im, s5_c_re, s5_c_im,
           s5_d, s5_log_dt, s5_w_glu, s5_b_glu, od_w_in, od_w_out, hg_lower, hg_norm_g):
    bsz, seq, d = x.shape
    depth = norm_g.shape[0]
    assert seq % DA_SPAN == 0 and d % LANES == 0
    lb_p = jax.nn.softmax(hg_lower.astype(f32), axis=0)
    lb_all = jnp.cumsum(lb_p, axis=0) - lb_p[0]
    ngrp = len(DA_PATTERNS)
    slopes = (2.0 ** (-8.0 * jnp.arange(1, ngrp * DA_HEADS + 1, dtype=f32) / (ngrp * DA_HEADS))
              ).reshape(ngrp, DA_HEADS)
    sd = slopes * jnp.asarray([float(p[1]) for p in DA_PATTERNS], f32)[:, None]

    h = x.reshape(bsz * seq, d)
    for layer in range(depth):
        j = layer // 2
        g = norm_g[layer]
        if layer % 2 == 0:
            (proj,) = _norm_proj(h, g[0], ev_w_in[j].astype(bf16), (("flat", ev_w_in.shape[2]),))
            ya, yb = _even_mixer(proj, bsz, seq, rg_conv_w[j], rg_conv_b[j], rg_w_a[j], rg_b_a[j],
                                 rg_w_x[j], rg_b_x[j], rg_lambda[j], s5_a_re[j], s5_a_im[j],
                                 s5_b_re[j], s5_b_im[j], s5_c_re[j], s5_c_im[j], s5_d[j], s5_log_dt[j],
                                 s5_w_glu[j], s5_b_glu[j])
            w_out = ev_w_out[j]
        else:
            hgp, qkv = _norm_proj(h, g[0], od_w_in[j].astype(bf16),
                                  (("flat", 4 * HG_W), ("slab", ngrp * 3 * DA_WIDTH)))
            ya = _hgrn_mixer(hgp, bsz, seq, lb_all[layer], hg_norm_g[j])
            yb = _attn_mixer(qkv, bsz, seq, sd)
            w_out = od_w_out[j]
        h = _post(ya, yb, h, bsz, seq, w_out, g, ffn_w_up[layer], ffn_conv_w[layer], ffn_conv_b[layer],
                  ffn_w_down[layer])
    return h.reshape(bsz, seq, d)
```

```python
import functools
import math

import jax
import jax.numpy as jnp
from jax import lax
from jax.experimental import pallas as pl
from jax.experimental.pallas import tpu as pltpu

f32 = jnp.float32
bf16 = jnp.bfloat16

EPS = 1e-6
NEG_INF = -1e30
RG_C = 8.0
RG_HEADS = 8
RG_WIDTH = 512
S5_WIDTH = 256
S5_GROUP = 16
S5_STATE = 64
HG_HEADS = 4
HG_DK = 128
HG_CHUNK = 64
HG_W = HG_HEADS * HG_DK
DA_HEADS = 4
DA_HEAD_DIM = 64
DA_WIDTH = DA_HEADS * DA_HEAD_DIM
DA_PATTERNS = ((128, 1), (512, 4), (2048, 16))
DA_BLOCK = 128
DA_SPAN = 2048
LANES = 128
SUBLANES = 8
VMEM_LIMIT = 56 * 1024 * 1024


def _rms(x, g):
    return x * lax.rsqrt(jnp.mean(x * x, axis=-1, keepdims=True) + EPS) * g


def _dot(a, b):
    return jnp.dot(a, b, preferred_element_type=f32)


def _dot_nt(a, b):
    return lax.dot_general(a, b, (((1,), (1,)), ((), ())), preferred_element_type=f32)


def _dot_tn(a, b):
    return lax.dot_general(a, b, (((0,), (0,)), ((), ())), preferred_element_type=f32)


def _const_spec(shape):
    nd = len(shape)
    return pl.BlockSpec(shape, lambda *_: (0,) * nd, pipeline_mode=pl.Buffered(1))


def _params(sem):
    return pltpu.CompilerParams(dimension_semantics=sem, vmem_limit_bytes=VMEM_LIMIT)


PROJ_TM = 256
PROJ_CHUNK = 512


def _norm_proj_kernel(x_ref, g_ref, w_ref, *out_refs, layout):
    xn = _rms(x_ref[...], g_ref[...]).astype(bf16)
    col = 0
    for o_ref, (kind, width) in zip(out_refs, layout):
        for c0 in range(0, width, PROJ_CHUNK):
            cw = min(PROJ_CHUNK, width - c0)
            res = _dot(xn, w_ref[:, col + c0:col + c0 + cw])
            if kind == "flat":
                o_ref[:, c0:c0 + cw] = res
            else:
                for s in range(cw // LANES):
                    o_ref[c0 // LANES + s] = res[:, s * LANES:(s + 1) * LANES]
        col += width


def _norm_proj(h2d, g, w, layout):
    n, d = h2d.shape
    tm = PROJ_TM
    out_shape, out_specs = [], []
    for kind, width in layout:
        if kind == "flat":
            out_shape.append(jax.ShapeDtypeStruct((n, width), f32))
            out_specs.append(pl.BlockSpec((tm, width), lambda i: (i, 0)))
        else:
            out_shape.append(jax.ShapeDtypeStruct((width // LANES, n, LANES), f32))
            out_specs.append(pl.BlockSpec((width // LANES, tm, LANES), lambda i: (0, i, 0)))
    return pl.pallas_call(
        functools.partial(_norm_proj_kernel, layout=layout),
        grid=(n // tm,),
        in_specs=[pl.BlockSpec((tm, d), lambda i: (i, 0)),
                  _const_spec((1, d)),
                  _const_spec(w.shape)],
        out_specs=out_specs,
        out_shape=out_shape,
        compiler_params=_params(("arbitrary",)),
        name="norm_proj",
    )(h2d, g.reshape(1, d), w)


EV_T = 256


def _even_kernel(p_ref, rcw_ref, rcb_ref, wa_ref, ba_ref, wx_ref, bx_ref, lam_ref,
                 bre_ref, bim_ref, lp_ref, pre_ref, pim_ref, cre_ref, cim_ref,
                 d5_ref, wg_ref, bg_ref,
                 ya_ref, yb_ref,
                 xst, rga, rgb, rgc, sre, sim, scr, sci):
    T = EV_T
    t = pl.program_id(1)

    @pl.when(t == 0)
    def _():
        xst[:, 0:SUBLANES, :] = jnp.zeros((RG_WIDTH // LANES, SUBLANES, LANES), f32)
        rgc[...] = jnp.zeros_like(rgc)
        scr[...] = jnp.zeros_like(scr)
        sci[...] = jnp.zeros_like(sci)

    kc = rcw_ref.shape[0]
    us = []
    for s in range(RG_WIDTH // LANES):
        cols = slice(s * LANES, (s + 1) * LANES)
        xst[s, SUBLANES:SUBLANES + T, :] = p_ref[:, cols]
        us_ = rcb_ref[:, cols]
        for k in range(kc):
            off = SUBLANES - (kc - 1) + k
            us_ = us_ + rcw_ref[k:k + 1, cols] * xst[s, off:off + T, :]
        us.append(us_)
    u = jnp.concatenate(us, axis=1)
    xst[:, 0:SUBLANES, :] = xst[:, T:T + SUBLANES, :]
    ub = u.astype(bf16)
    r = jax.nn.sigmoid(_dot(ub, wa_ref[...]) + ba_ref[...])
    ig = jax.nn.sigmoid(_dot(ub, wx_ref[...]) + bx_ref[...])
    log_a = (RG_C * r) * jax.nn.log_sigmoid(lam_ref[...])
    a = jnp.exp(log_a)
    bt = jnp.sqrt(-jnp.tanh(log_a) * (a * a + 1.0)) * (ig * u)
    ngrp = T // SUBLANES
    a = a.reshape(ngrp, SUBLANES, RG_WIDTH)
    bt = bt.reshape(ngrp, SUBLANES, RG_WIDTH)
    rowm = lax.broadcasted_iota(jnp.int32, (1, SUBLANES, RG_WIDTH), 1)
    for s in (1, 2, 4):
        m = rowm >= s
        bt = bt + jnp.where(m, a * pltpu.roll(bt, s, 1), 0.0)
        a = a * jnp.where(m, pltpu.roll(a, s, 1), 1.0)
    rga[...] = a.reshape(T, RG_WIDTH)
    rgb[...] = bt.reshape(T, RG_WIDTH)

    def rg_body(gidx, c):
        r0 = pl.multiple_of(gidx * SUBLANES, SUBLANES)
        h = rga[pl.ds(r0, SUBLANES), :] * c + rgb[pl.ds(r0, SUBLANES), :]
        rgb[pl.ds(r0, SUBLANES), :] = h
        return jnp.broadcast_to(h[SUBLANES - 1:SUBLANES, :], (SUBLANES, RG_WIDTH))

    rgc[...] = lax.fori_loop(0, T // SUBLANES, rg_body, rgc[...], unroll=4)
    ya_ref[...] = rgb[...] * jax.nn.gelu(p_ref[:, RG_WIDTH:2 * RG_WIDTH])

    u5 = p_ref[:, 2 * RG_WIDTH:2 * RG_WIDTH + S5_WIDTH]
    u5b = u5.astype(bf16)
    hre = _dot(u5b, bre_ref[...])
    him = _dot(u5b, bim_ref[...])
    ns = hre.shape[1]
    hre = hre.reshape(ngrp, SUBLANES, ns)
    him = him.reshape(ngrp, SUBLANES, ns)
    for i, s in enumerate((1, 2, 4)):
        cr = lp_ref[2 * i][None]
        ci = lp_ref[2 * i + 1][None]
        zr = pltpu.roll(hre, s, 1)
        zi = pltpu.roll(him, s, 1)
        hre, him = hre + (cr * zr - ci * zi), him + (cr * zi + ci * zr)
    sre[...] = hre.reshape(T, ns)
    sim[...] = him.reshape(T, ns)

    def s5_body(gidx, c):
        cre, cim = c
        r0 = pl.multiple_of(gidx * SUBLANES, SUBLANES)
        pr = pre_ref[...]
        pi = pim_ref[...]
        hr = sre[pl.ds(r0, SUBLANES), :] + (pr * cre - pi * cim)
        hi = sim[pl.ds(r0, SUBLANES), :] + (pr * cim + pi * cre)
        sre[pl.ds(r0, SUBLANES), :] = hr
        sim[pl.ds(r0, SUBLANES), :] = hi
        return (jnp.broadcast_to(hr[SUBLANES - 1:SUBLANES, :], (SUBLANES, ns)),
                jnp.broadcast_to(hi[SUBLANES - 1:SUBLANES, :], (SUBLANES, ns)))

    cre, cim = lax.fori_loop(0, T // SUBLANES, s5_body, (scr[...], sci[...]), unroll=2)
    scr[...] = cre
    sci[...] = cim
    y5 = (_dot(sre[...].astype(bf16), cre_ref[...]) - _dot(sim[...].astype(bf16), cim_ref[...])
          + d5_ref[...] * u5)
    v = jax.nn.gelu(y5)
    yb_ref[...] = v * jax.nn.sigmoid(_dot(v.astype(bf16), wg_ref[...]) + bg_ref[...])


def _block_diag(w):
    g, a, b = w.shape
    eye = jnp.eye(g, dtype=w.dtype)
    return (eye[:, None, :, None] * w[:, :, None, :]).reshape(g * a, g * b)


def _even_mixer(proj, bsz, seq, rg_conv_w, rg_conv_b, rg_w_a, rg_b_a, rg_w_x, rg_b_x, rg_lambda,
                s5_a_re, s5_a_im, s5_b_re, s5_b_im, s5_c_re, s5_c_im, s5_d, s5_log_dt,
                s5_w_glu, s5_b_glu):
    n = proj.shape[0]
    T = EV_T
    nt = seq // T
    ns = s5_a_re.shape[0] * S5_STATE
    dt = jnp.exp(s5_log_dt.astype(f32))[:, None]
    ar = s5_a_re.astype(f32)
    ai = s5_a_im.astype(f32)
    mag = jnp.exp(ar * dt)
    abar_re = mag * jnp.cos(ai * dt)
    abar_im = mag * jnp.sin(ai * dt)
    den = ar * ar + ai * ai
    num_re = abar_re - 1.0
    f_re = (num_re * ar + abar_im * ai) / den
    f_im = (abar_im * ar - num_re * ai) / den
    br = s5_b_re.astype(f32)
    bi = s5_b_im.astype(f32)
    bb_re = f_re[..., None] * br - f_im[..., None] * bi
    bb_im = f_re[..., None] * bi + f_im[..., None] * br
    bre_bd = _block_diag(jnp.swapaxes(bb_re, 1, 2)).astype(bf16)
    bim_bd = _block_diag(jnp.swapaxes(bb_im, 1, 2)).astype(bf16)
    cre_bd = _block_diag(jnp.swapaxes(s5_c_re.astype(f32), 1, 2)).astype(bf16)
    cim_bd = _block_diag(jnp.swapaxes(s5_c_im.astype(f32), 1, 2)).astype(bf16)
    l1r = abar_re.reshape(1, ns)
    l1i = abar_im.reshape(1, ns)
    pows = [(l1r, l1i)]
    for _ in range(SUBLANES - 1):
        pr, pi = pows[-1]
        pows.append((pr * l1r - pi * l1i, pr * l1i + pi * l1r))
    rows = jnp.arange(SUBLANES)[:, None]
    lp = jnp.stack([jnp.where(rows >= s, pows[s - 1][c], 0.0) for s in (1, 2, 4) for c in (0, 1)])
    p_re = jnp.concatenate([p[0] for p in pows], axis=0)
    p_im = jnp.concatenate([p[1] for p in pows], axis=0)
    wa_bd = _block_diag(rg_w_a).astype(bf16)
    wx_bd = _block_diag(rg_w_x).astype(bf16)
    row = lambda v: v.reshape(1, -1).astype(f32)
    consts = [rg_conv_w.astype(f32), row(rg_conv_b), wa_bd, row(rg_b_a), wx_bd, row(rg_b_x), row(rg_lambda),
              bre_bd, bim_bd, lp, p_re, p_im, cre_bd, cim_bd, row(s5_d),
              s5_w_glu.astype(bf16), row(s5_b_glu)]
    width = proj.shape[1]
    return pl.pallas_call(
        _even_kernel,
        grid=(bsz, nt),
        in_specs=[pl.BlockSpec((T, width), lambda b, t: (b * nt + t, 0))]
                 + [_const_spec(c.shape) for c in consts],
        out_specs=[pl.BlockSpec((T, RG_WIDTH), lambda b, t: (b * nt + t, 0)),
                   pl.BlockSpec((T, S5_WIDTH), lambda b, t: (b * nt + t, 0))],
        out_shape=[jax.ShapeDtypeStruct((n, RG_WIDTH), f32),
                   jax.ShapeDtypeStruct((n, S5_WIDTH), f32)],
        scratch_shapes=[pltpu.VMEM((RG_WIDTH // LANES, T + SUBLANES, LANES), f32),
                        pltpu.VMEM((T, RG_WIDTH), f32), pltpu.VMEM((T, RG_WIDTH), f32),
                        pltpu.VMEM((SUBLANES, RG_WIDTH), f32),
                        pltpu.VMEM((T, ns), f32), pltpu.VMEM((T, ns), f32),
                        pltpu.VMEM((SUBLANES, ns), f32), pltpu.VMEM((SUBLANES, ns), f32)],
        compiler_params=_params(("arbitrary", "arbitrary")),
        name="even_mixer",
    )(proj, *consts)


HG_T = 512
HG_UNROLL = 4


def _split3(x):
    hi = x.astype(bf16)
    r1 = x - hi.astype(f32)
    mid = r1.astype(bf16)
    lo = (r1 - mid.astype(f32)).astype(bf16)
    return hi, mid, lo


def _hgrn_kernel(p_ref, lb_ref, nw_ref, o_ref, st_ref):
    C = HG_CHUNK
    t = pl.program_id(1)

    @pl.when(t == 0)
    def _():
        st_ref[...] = jnp.zeros_like(st_ref)

    ri = lax.broadcasted_iota(jnp.int32, (C, C), 0)
    ci = lax.broadcasted_iota(jnp.int32, (C, C), 1)
    causal = ri >= ci
    tril = causal.astype(bf16)
    lb = lb_ref[...]

    def chunk(c, carry):
        r0 = pl.multiple_of(c * C, C)
        q = p_ref[pl.ds(r0, C), 0:HG_W]
        z = p_ref[pl.ds(r0, C), HG_W:2 * HG_W]
        log_f = jnp.log(lb + (1.0 - lb) * jax.nn.sigmoid(z))
        kk = (1.0 - lb) * jax.nn.sigmoid(-z)
        hi, mid_, lo = _split3(log_f)
        cum = _dot(tril, hi) + _dot(tril, mid_) + _dot(tril, lo)
        qs = jax.nn.silu(q)
        for h in range(HG_HEADS):
            sl = slice(h * HG_DK, (h + 1) * HG_DK)
            cum_h = cum[:, sl]
            midv = cum_h[C // 2:C // 2 + 1, :]
            last = cum_h[C - 1:C, :]
            q_h = qs[:, sl]
            k_h = kk[:, sl]
            v_h = p_ref[pl.ds(r0, C), 2 * HG_W + h * HG_DK:2 * HG_W + (h + 1) * HG_DK].astype(bf16)
            g_h = p_ref[pl.ds(r0, C), 3 * HG_W + h * HG_DK:3 * HG_W + (h + 1) * HG_DK]
            sc = _dot_nt((q_h * jnp.exp(cum_h - midv)).astype(bf16),
                         (k_h * jnp.exp(midv - cum_h)).astype(bf16))
            sc = jnp.where(causal, sc, 0.0)
            o = _dot(sc.astype(bf16), v_h)
            st = st_ref[h]
            o = o + _dot_nt((q_h * jnp.exp(cum_h)).astype(bf16), st.astype(bf16))
            con_t = _dot_tn(v_h, (k_h * jnp.exp(last - cum_h)).astype(bf16))
            st_ref[h] = jnp.exp(last) * st + con_t
            on = o * lax.rsqrt(jnp.mean(o * o, axis=-1, keepdims=True) + EPS) * nw_ref[...]
            o_ref[pl.ds(r0, C), sl] = on * jax.nn.silu(g_h)
        return carry

    lax.fori_loop(0, HG_T // C, chunk, 0, unroll=HG_UNROLL)


def _hgrn_mixer(hg, bsz, seq, lb, norm_w):
    n = hg.shape[0]
    T = HG_T
    nt = seq // T
    return pl.pallas_call(
        _hgrn_kernel,
        grid=(bsz, nt),
        in_specs=[pl.BlockSpec((T, 4 * HG_W), lambda b, t: (b * nt + t, 0)),
                  _const_spec((1, HG_W)), _const_spec((1, HG_DK))],
        out_specs=pl.BlockSpec((T, HG_W), lambda b, t: (b * nt + t, 0)),
        out_shape=jax.ShapeDtypeStruct((n, HG_W), f32),
        scratch_shapes=[pltpu.VMEM((HG_HEADS, HG_DK, HG_DK), f32)],
        compiler_params=_params(("arbitrary", "arbitrary")),
        name="hgrn_mixer",
    )(hg, lb.reshape(1, HG_W).astype(f32), norm_w.reshape(1, HG_DK).astype(f32))


N_PAIR = DA_WIDTH // LANES
DA_MERGE_ROWS = 256


def _attn_group(gi, d, n, sd_ref, x_ref, qd, kv, og, lg, bias_scr):
    B = DA_BLOCK
    L = DA_SPAN // d
    nblk = L // B
    nblk_log = int(math.log2(nblk))
    stride_r = L + B

    @pl.when(n == 0)
    def _():
        for r in range(d):
            kv[:, r * stride_r:r * stride_r + B, :] = jnp.zeros((2 * N_PAIR, B, LANES), bf16)

    scale = DA_HEAD_DIM ** -0.5
    for r in range(d):
        rows = pl.ds(r, L, stride=d) if d > 1 else pl.ds(0, L)
        for s in range(N_PAIR):
            qd[s, r * L:(r + 1) * L, :] = (x_ref[s, rows, :] * scale).astype(bf16)
        for s in range(2 * N_PAIR):
            kv[s, r * stride_r + B:r * stride_r + B + L, :] = x_ref[N_PAIR + s, rows, :].astype(bf16)

    qi = lax.broadcasted_iota(jnp.int32, (B, 2 * B), 0)
    kj = lax.broadcasted_iota(jnp.int32, (B, 2 * B), 1)
    rel = qi + B - kj
    valid = (rel >= 0) & (rel <= B)
    relf = rel.astype(f32)
    for h in range(DA_HEADS):
        bias = jnp.where(valid, -sd_ref[gi, h] * relf, NEG_INF)
        bias_scr[0, h] = bias
        bias_scr[1, h] = jnp.where(kj < B, NEG_INF, bias)

    lane = lax.broadcasted_iota(jnp.int32, (B, LANES), 1)
    lo_half = lane < DA_HEAD_DIM

    def blk_body(blk, carry):
        r = blk >> nblk_log
        m = blk & (nblk - 1)
        q0 = pl.multiple_of(blk * B, B)
        k0 = pl.multiple_of((blk + r) * B, B)
        first = jnp.logical_and(n == 0, m == 0).astype(jnp.int32)
        start = m * (B * d) + r
        orow = pl.ds(start, B, stride=d) if d > 1 else pl.ds(pl.multiple_of(start, B), B)
        for j in range(N_PAIR):
            qp = qd[j, pl.ds(q0, B), :]
            kp = kv[j, pl.ds(k0, 2 * B), :]
            vp = kv[N_PAIR + j, pl.ds(k0, 2 * B), :]
            outs, lses = [], []
            for e in range(2):
                keep = lo_half if e == 0 else jnp.logical_not(lo_half)
                qm = jnp.where(keep, qp, jnp.zeros_like(qp))
                s = _dot_nt(qm, kp) + bias_scr[first, 2 * j + e]
                mx = jnp.max(s, axis=-1, keepdims=True)
                ex = jnp.exp(s - mx)
                l = jnp.sum(ex, axis=-1, keepdims=True)
                outs.append(_dot(ex.astype(bf16), vp) * (1.0 / l))
                lses.append(jnp.broadcast_to(mx + jnp.log(l), (B, LANES)))
            og[gi, j, orow, :] = jnp.where(lo_half, outs[0], outs[1])
            lg[gi, j, orow, :] = jnp.where(lo_half, lses[0], lses[1])
        return carry

    lax.fori_loop(0, DA_SPAN // B, blk_body, 0, unroll=8)

    for r in range(d):
        kv[:, r * stride_r:r * stride_r + B, :] = kv[:, r * stride_r + L:r * stride_r + L + B, :]


def _attn_kernel(sd_ref, x_ref, o_ref, qd, kv0, kv1, kv2, og, lg, bias_scr):
    n = pl.program_id(1)
    g = pl.program_id(2)
    kvs = (kv0, kv1, kv2)
    for gi, (_, d) in enumerate(DA_PATTERNS):
        @pl.when(g == gi)
        def _(gi=gi, d=d):
            _attn_group(gi, d, n, sd_ref, x_ref, qd, kvs[gi], og, lg, bias_scr)

    @pl.when(g == len(DA_PATTERNS) - 1)
    def _():
        R = DA_MERGE_ROWS

        def merge(i, carry):
            r0 = pl.multiple_of(i * R, R)
            for j in range(N_PAIR):
                ls = [lg[gi, j, pl.ds(r0, R), :] for gi in range(len(DA_PATTERNS))]
                mx = functools.reduce(jnp.maximum, ls)
                ws = [jnp.exp(l - mx) for l in ls]
                den = functools.reduce(lambda a, b: a + b, ws)
                num = functools.reduce(lambda a, b: a + b,
                                       [w * og[gi, j, pl.ds(r0, R), :] for gi, w in enumerate(ws)])
                o_ref[pl.ds(r0, R), j * LANES:(j + 1) * LANES] = num * (1.0 / den)
            return carry

        lax.fori_loop(0, DA_SPAN // R, merge, 0)


def _attn_mixer(qkv_slabs, bsz, seq, sd):
    ng = len(DA_PATTERNS)
    n = qkv_slabs.shape[1]
    nt = seq // DA_SPAN
    per_g = 3 * N_PAIR
    kv_scratch = [pltpu.VMEM((2 * N_PAIR, DA_SPAN + DA_BLOCK * d, LANES), bf16) for _, d in DA_PATTERNS]
    return pl.pallas_call(
        _attn_kernel,
        grid=(bsz, nt, ng),
        in_specs=[pl.BlockSpec(memory_space=pltpu.SMEM),
                  pl.BlockSpec((per_g, DA_SPAN, LANES), lambda b, t, g: (g, b * nt + t, 0))],
        out_specs=pl.BlockSpec((DA_SPAN, DA_WIDTH), lambda b, t, g: (b * nt + t, 0)),
        out_shape=jax.ShapeDtypeStruct((n, DA_WIDTH), f32),
        scratch_shapes=[pltpu.VMEM((N_PAIR, DA_SPAN, LANES), bf16)] + kv_scratch + [
            pltpu.VMEM((ng, N_PAIR, DA_SPAN, LANES), f32),
            pltpu.VMEM((ng, N_PAIR, DA_SPAN, LANES), f32),
            pltpu.VMEM((2, DA_HEADS, DA_BLOCK, 2 * DA_BLOCK), f32)],
        compiler_params=_params(("arbitrary", "arbitrary", "arbitrary")),
        name="dilated_attn",
    )(sd, qkv_slabs)


POST_TM = 256
FF_CHUNK = 256


def _post_kernel(ya_ref, yb_ref, h_ref, woa_ref, wob_ref, g_ref, wup_ref, cw_ref, cb_ref, wdn_ref,
                 o_ref, xn_scr, st_scr, carry_scr, acc_scr, act_scr):
    tm = POST_TM
    C = FF_CHUNK
    dff = wdn_ref.shape[0]
    kc = cw_ref.shape[0]
    t = pl.program_id(1)

    @pl.when(t == 0)
    def _():
        carry_scr[...] = jnp.zeros_like(carry_scr)

    y1 = _dot(ya_ref[...].astype(bf16), woa_ref[...]) + _dot(yb_ref[...].astype(bf16), wob_ref[...])
    h1 = h_ref[...] + _rms(y1, g_ref[1:2, :])
    o_ref[...] = h1
    xn_scr[...] = _rms(h1, g_ref[2:3, :]).astype(bf16)
    acc_scr[...] = jnp.zeros_like(acc_scr)
    nchunk = dff // C

    ns = C // LANES

    def up(j):
        slot = j % 2
        xn = xn_scr[...]
        c0 = j * C
        rv = _dot(xn, wup_ref[:, c0:c0 + C])
        rg = _dot(xn, wup_ref[:, dff + c0:dff + c0 + C])
        st_scr[slot, :, 0:SUBLANES, :] = carry_scr[j]
        for s in range(ns):
            st_scr[slot, s, SUBLANES:SUBLANES + tm, :] = rv[:, s * LANES:(s + 1) * LANES]
            st_scr[slot, ns + s, SUBLANES:SUBLANES + tm, :] = rg[:, s * LANES:(s + 1) * LANES]

    def act(j):
        slot = j % 2
        parts = []
        for s in range(ns):
            c0 = j * C + s * LANES
            cv = cb_ref[:, c0:c0 + LANES]
            cg = cb_ref[:, dff + c0:dff + c0 + LANES]
            for k in range(kc):
                off = SUBLANES - (kc - 1) + k
                cv = cv + cw_ref[k:k + 1, c0:c0 + LANES] * st_scr[slot, s, off:off + tm, :]
                cg = cg + cw_ref[k:k + 1, dff + c0:dff + c0 + LANES] * st_scr[slot, ns + s, off:off + tm, :]
            parts.append((jax.nn.gelu(cg) * cv).astype(bf16))
        act_scr[slot] = jnp.concatenate(parts, axis=1)
        carry_scr[j] = st_scr[slot, :, tm:tm + SUBLANES, :]

    def down(j):
        acc_scr[...] += _dot(act_scr[j % 2], wdn_ref[j * C:(j + 1) * C, :])

    up(0)
    for j in range(nchunk):
        if j + 1 < nchunk:
            up(j + 1)
        if j >= 1:
            down(j - 1)
        act(j)
    down(nchunk - 1)
    o_ref[...] = o_ref[...] + _rms(acc_scr[...], g_ref[3:4, :])


def _post(ya, yb, h2d, bsz, seq, w_out, g, w_up, conv_w, conv_b, w_down):
    n, d = h2d.shape
    tm = POST_TM
    nt = seq // tm
    wa = ya.shape[1]
    wb = yb.shape[1]
    dff = w_down.shape[0]
    woa = w_out[:wa].astype(bf16)
    wob = w_out[wa:].astype(bf16)
    consts = [woa, wob, g.astype(f32), w_up.astype(bf16), conv_w.astype(f32),
              conv_b.reshape(1, -1).astype(f32), w_down.astype(bf16)]
    row_spec = lambda w: pl.BlockSpec((tm, w), lambda b, t: (b * nt + t, 0))
    return pl.pallas_call(
        _post_kernel,
        grid=(bsz, nt),
        in_specs=[row_spec(wa), row_spec(wb), row_spec(d)] + [_const_spec(c.shape) for c in consts],
        out_specs=row_spec(d),
        out_shape=jax.ShapeDtypeStruct((n, d), f32),
        scratch_shapes=[pltpu.VMEM((tm, d), bf16),
                        pltpu.VMEM((2, 2 * FF_CHUNK // LANES, tm + SUBLANES, LANES), f32),
                        pltpu.VMEM((dff // FF_CHUNK, 2 * FF_CHUNK // LANES, SUBLANES, LANES), f32),
                        pltpu.VMEM((tm, d), f32),
                        pltpu.VMEM((2, tm, FF_CHUNK), bf16)],
        compiler_params=_params(("arbitrary", "arbitrary")),
        name="post_ffn",
    )(ya, yb, h2d, *consts)


def kernel(x, norm_g, ffn_w_up, ffn_conv_w, ffn_conv_b, ffn_w_down, ev_w_in, ev_w_out, rg_conv_w, rg_conv_b,
           rg_w_a, rg_b_a, rg_w_x, rg_b_x, rg_lambda, s5_a_re, s5_a_im, s5_b_re, s5_b_im, s5_c_re, s5_c_im,
           s5_d, s5_log_dt, s5_w_glu, s5_b_glu, od_w_in, od_w_out, hg_lower, hg_norm_g):
    bsz, seq, d = x.shape
    depth = norm_g.shape[0]
    assert seq % DA_SPAN == 0 and d % LANES == 0
    lb_p = jax.nn.softmax(hg_lower.astype(f32), axis=0)
    lb_all = jnp.cumsum(lb_p, axis=0) - lb_p[0]
    ngrp = len(DA_PATTERNS)
    slopes = (2.0 ** (-8.0 * jnp.arange(1, ngrp * DA_HEADS + 1, dtype=f32) / (ngrp * DA_HEADS))
              ).reshape(ngrp, DA_HEADS)
    sd = slopes * jnp.asarray([float(p[1]) for p in DA_PATTERNS], f32)[:, None]

    h = x.reshape(bsz * seq, d)
    for layer in range(depth):
        j = layer // 2
        g = norm_g[layer]
        if layer % 2 == 0:
            (proj,) = _norm_proj(h, g[0], ev_w_in[j].astype(bf16), (("flat", ev_w_in.shape[2]),))
            ya, yb = _even_mixer(proj, bsz, seq, rg_conv_w[j], rg_conv_b[j], rg_w_a[j], rg_b_a[j],
                                 rg_w_x[j], rg_b_x[j], rg_lambda[j], s5_a_re[j], s5_a_im[j],
                                 s5_b_re[j], s5_b_im[j], s5_c_re[j], s5_c_im[j], s5_d[j], s5_log_dt[j],
                                 s5_w_glu[j], s5_b_glu[j])
            w_out = ev_w_out[j]
        else:
            hgp, qkv = _norm_proj(h, g[0], od_w_in[j].astype(bf16),
                                  (("flat", 4 * HG_W), ("slab", ngrp * 3 * DA_WIDTH)))
            ya = _hgrn_mixer(hgp, bsz, seq, lb_all[layer], hg_norm_g[j])
            yb = _attn_mixer(qkv, bsz, seq, sd)
            w_out = od_w_out[j]
        h = _post(ya, yb, h, bsz, seq, w_out, g, ffn_w_up[layer], ffn_conv_w[layer], ffn_conv_b[layer],
                  ffn_w_down[layer])
    return h.reshape(bsz, seq, d)
```

```python
import functools
import math

import jax
import jax.numpy as jnp
from jax import lax
from jax.experimental import pallas as pl
from jax.experimental.pallas import tpu as pltpu

f32 = jnp.float32
bf16 = jnp.bfloat16

EPS = 1e-6
NEG_INF = -1e30
RG_C = 8.0
RG_HEADS = 8
RG_WIDTH = 512
S5_WIDTH = 256
S5_GROUP = 16
S5_STATE = 64
HG_HEADS = 4
HG_DK = 128
HG_CHUNK = 64
HG_W = HG_HEADS * HG_DK
DA_HEADS = 4
DA_HEAD_DIM = 64
DA_WIDTH = DA_HEADS * DA_HEAD_DIM
DA_PATTERNS = ((128, 1), (512, 4), (2048, 16))
DA_BLOCK = 128
DA_SPAN = 2048
LANES = 128
SUBLANES = 8
VMEM_LIMIT = 56 * 1024 * 1024


def _rms(x, g):
    return x * lax.rsqrt(jnp.mean(x * x, axis=-1, keepdims=True) + EPS) * g


def _dot(a, b):
    return jnp.dot(a, b, preferred_element_type=f32)


def _dot_nt(a, b):
    return lax.dot_general(a, b, (((1,), (1,)), ((), ())), preferred_element_type=f32)


def _dot_tn(a, b):
    return lax.dot_general(a, b, (((0,), (0,)), ((), ())), preferred_element_type=f32)


def _const_spec(shape):
    nd = len(shape)
    return pl.BlockSpec(shape, lambda *_: (0,) * nd, pipeline_mode=pl.Buffered(1))


def _params(sem):
    return pltpu.CompilerParams(dimension_semantics=sem, vmem_limit_bytes=VMEM_LIMIT)


PROJ_TM = 256
PROJ_CHUNK = 512


def _norm_proj_kernel(x_ref, g_ref, w_ref, *out_refs, layout):
    xn = _rms(x_ref[...], g_ref[...]).astype(bf16)
    col = 0
    for o_ref, (kind, width) in zip(out_refs, layout):
        for c0 in range(0, width, PROJ_CHUNK):
            cw = min(PROJ_CHUNK, width - c0)
            res = _dot(xn, w_ref[:, col + c0:col + c0 + cw])
            if kind == "flat":
                o_ref[:, c0:c0 + cw] = res
            else:
                for s in range(cw // LANES):
                    o_ref[c0 // LANES + s] = res[:, s * LANES:(s + 1) * LANES]
        col += width


def _norm_proj(h2d, g, w, layout):
    n, d = h2d.shape
    tm = PROJ_TM
    out_shape, out_specs = [], []
    for kind, width in layout:
        if kind == "flat":
            out_shape.append(jax.ShapeDtypeStruct((n, width), f32))
            out_specs.append(pl.BlockSpec((tm, width), lambda i: (i, 0)))
        else:
            out_shape.append(jax.ShapeDtypeStruct((width // LANES, n, LANES), f32))
            out_specs.append(pl.BlockSpec((width // LANES, tm, LANES), lambda i: (0, i, 0)))
    return pl.pallas_call(
        functools.partial(_norm_proj_kernel, layout=layout),
        grid=(n // tm,),
        in_specs=[pl.BlockSpec((tm, d), lambda i: (i, 0)),
                  _const_spec((1, d)),
                  _const_spec(w.shape)],
        out_specs=out_specs,
        out_shape=out_shape,
        compiler_params=_params(("arbitrary",)),
        name="norm_proj",
    )(h2d, g.reshape(1, d), w)


EV_T = 512
S5_L = 4
EV_PROJ_CHUNK = 256


def _even_kernel(h_ref, g_ref, win_ref, rcw_ref, rcb_ref, wa_ref, ba_ref, wx_ref, bx_ref, lam_ref,
                 ks_ref, bs_ref, ms_ref, lp_ref, pre_ref, pim_ref,
                 d5_ref, wg_ref, bg_ref,
                 ya_ref, yb_ref,
                 pj, xst, rga, rgb, rgc, ust, sre, sim, scr, sci, hst, yst):
    T = EV_T
    t = pl.program_id(1)

    @pl.when(t == 0)
    def _():
        xst[:, 0:SUBLANES, :] = jnp.zeros((RG_WIDTH // LANES, SUBLANES, LANES), f32)
        ust[:, 0:SUBLANES, :] = jnp.zeros((S5_WIDTH // LANES, SUBLANES, LANES), f32)
        rgc[...] = jnp.zeros_like(rgc)
        scr[...] = jnp.zeros_like(scr)
        sci[...] = jnp.zeros_like(sci)

    xn = _rms(h_ref[...], g_ref[...]).astype(bf16)
    for c0 in range(0, pj.shape[1], EV_PROJ_CHUNK):
        pj[:, c0:c0 + EV_PROJ_CHUNK] = _dot(xn, win_ref[:, c0:c0 + EV_PROJ_CHUNK])
    p_ref = pj

    kc = rcw_ref.shape[0]
    us = []
    for s in range(RG_WIDTH // LANES):
        cols = slice(s * LANES, (s + 1) * LANES)
        xst[s, SUBLANES:SUBLANES + T, :] = p_ref[:, cols]
        us_ = rcb_ref[:, cols]
        for k in range(kc):
            off = SUBLANES - (kc - 1) + k
            us_ = us_ + rcw_ref[k:k + 1, cols] * xst[s, off:off + T, :]
        us.append(us_)
    u = jnp.concatenate(us, axis=1)
    xst[:, 0:SUBLANES, :] = xst[:, T:T + SUBLANES, :]
    ub = u.astype(bf16)
    r = jax.nn.sigmoid(_dot(ub, wa_ref[...]) + ba_ref[...])
    ig = jax.nn.sigmoid(_dot(ub, wx_ref[...]) + bx_ref[...])
    log_a = (RG_C * r) * jax.nn.log_sigmoid(lam_ref[...])
    a = jnp.exp(log_a)
    bt = jnp.sqrt(-jnp.tanh(log_a) * (a * a + 1.0)) * (ig * u)
    ngrp = T // SUBLANES
    a = a.reshape(ngrp, SUBLANES, RG_WIDTH)
    bt = bt.reshape(ngrp, SUBLANES, RG_WIDTH)
    rowm = lax.broadcasted_iota(jnp.int32, (1, SUBLANES, RG_WIDTH), 1)
    for s in (1, 2, 4):
        m = rowm >= s
        bt = bt + jnp.where(m, a * pltpu.roll(bt, s, 1), 0.0)
        a = a * jnp.where(m, pltpu.roll(a, s, 1), 1.0)
    rga[...] = a.reshape(T, RG_WIDTH)
    rgb[...] = bt.reshape(T, RG_WIDTH)
    ya_ref[...] = jax.nn.gelu(p_ref[:, RG_WIDTH:2 * RG_WIDTH])

    L = S5_L
    nc = T // L
    ns = sre.shape[1]
    nsl = S5_WIDTH // LANES
    u5 = p_ref[:, 2 * RG_WIDTH:2 * RG_WIDTH + S5_WIDTH]
    for s in range(nsl):
        ust[s, SUBLANES:SUBLANES + T, :] = u5[:, s * LANES:(s + 1) * LANES]
    rowl = lax.broadcasted_iota(jnp.int32, (T, LANES), 0) & (L - 1)
    pieces, ends = [], []
    for k in range(L):
        for s in range(nsl):
            p = ust[s, SUBLANES - k:SUBLANES - k + T, :]
            if k:
                p = jnp.where(rowl >= k, p, 0.0)
            pieces.append(p.astype(bf16))
            ends.append(ust[s, pl.ds(SUBLANES + L - 1 - k, nc, stride=L), :].astype(bf16))
    yb_ref[...] = _dot(jnp.concatenate(pieces, axis=1), ks_ref[...]) + d5_ref[...] * u5
    gend = _dot(jnp.concatenate(ends, axis=1), bs_ref[...])
    ngrp5 = nc // SUBLANES
    hre = gend[:, 0:ns].reshape(ngrp5, SUBLANES, ns)
    him = gend[:, ns:2 * ns].reshape(ngrp5, SUBLANES, ns)
    for i, s in enumerate((1, 2, 4)):
        cr = lp_ref[2 * i][None]
        ci = lp_ref[2 * i + 1][None]
        zr = pltpu.roll(hre, s, 1)
        zi = pltpu.roll(him, s, 1)
        hre, him = hre + (cr * zr - ci * zi), him + (cr * zi + ci * zr)
    sre[...] = hre.reshape(nc, ns)
    sim[...] = him.reshape(nc, ns)
    hst[0:SUBLANES, 0:ns] = scr[...]
    hst[0:SUBLANES, ns:2 * ns] = sci[...]

    def rg_body(gidx, c):
        r0 = pl.multiple_of(gidx * SUBLANES, SUBLANES)
        h = rga[pl.ds(r0, SUBLANES), :] * c + rgb[pl.ds(r0, SUBLANES), :]
        rgb[pl.ds(r0, SUBLANES), :] = h
        return jnp.broadcast_to(h[SUBLANES - 1:SUBLANES, :], (SUBLANES, RG_WIDTH))

    rgc[...] = lax.fori_loop(0, T // SUBLANES, rg_body, rgc[...], unroll=4)

    def s5_body(gidx, c):
        cre, cim = c
        r0 = pl.multiple_of(gidx * SUBLANES, SUBLANES)
        pr = pre_ref[...]
        pi = pim_ref[...]
        hr = sre[pl.ds(r0, SUBLANES), :] + (pr * cre - pi * cim)
        hi = sim[pl.ds(r0, SUBLANES), :] + (pr * cim + pi * cre)
        sre[pl.ds(r0, SUBLANES), :] = hr
        sim[pl.ds(r0, SUBLANES), :] = hi
        return (jnp.broadcast_to(hr[SUBLANES - 1:SUBLANES, :], (SUBLANES, ns)),
                jnp.broadcast_to(hi[SUBLANES - 1:SUBLANES, :], (SUBLANES, ns)))

    cre, cim = lax.fori_loop(0, ngrp5, s5_body, (scr[...], sci[...]), unroll=2)
    scr[...] = cre
    sci[...] = cim
    hst[SUBLANES:SUBLANES + nc, 0:ns] = sre[...]
    hst[SUBLANES:SUBLANES + nc, ns:2 * ns] = sim[...]
    hprev = hst[SUBLANES - 1:SUBLANES - 1 + nc, :].astype(bf16)
    yi = _dot(hprev, ms_ref[...])
    for r in range(L):
        for s in range(nsl):
            c0 = r * S5_WIDTH + s * LANES
            yst[s, pl.ds(r, nc, stride=L), :] = yi[:, c0:c0 + LANES]
    ya_ref[...] = ya_ref[...] * rgb[...]
    y5 = yb_ref[...] + jnp.concatenate([yst[s] for s in range(nsl)], axis=1)
    v = jax.nn.gelu(y5)
    yb_ref[...] = v * jax.nn.sigmoid(_dot(v.astype(bf16), wg_ref[...]) + bg_ref[...])


def _block_diag(w):
    g, a, b = w.shape
    eye = jnp.eye(g, dtype=w.dtype)
    return (eye[:, None, :, None] * w[:, :, None, :]).reshape(g * a, g * b)


def _even_mixer(h2d, g0, w_in, bsz, seq, rg_conv_w, rg_conv_b, rg_w_a, rg_b_a, rg_w_x, rg_b_x, rg_lambda,
                s5_a_re, s5_a_im, s5_b_re, s5_b_im, s5_c_re, s5_c_im, s5_d, s5_log_dt,
                s5_w_glu, s5_b_glu):
    n, dm = h2d.shape
    T = EV_T
    L = S5_L
    nt = seq // T
    ns = s5_a_re.shape[0] * S5_STATE
    dt = jnp.exp(s5_log_dt.astype(f32))[:, None]
    ar = s5_a_re.astype(f32)
    ai = s5_a_im.astype(f32)
    mag = jnp.exp(ar * dt)
    abar_re = mag * jnp.cos(ai * dt)
    abar_im = mag * jnp.sin(ai * dt)
    den = ar * ar + ai * ai
    num_re = abar_re - 1.0
    f_re = (num_re * ar + abar_im * ai) / den
    f_im = (abar_im * ar - num_re * ai) / den
    br = s5_b_re.astype(f32)
    bi = s5_b_im.astype(f32)
    bb_re = f_re[..., None] * br - f_im[..., None] * bi
    bb_im = f_re[..., None] * bi + f_im[..., None] * br
    cr = s5_c_re.astype(f32)
    ci = s5_c_im.astype(f32)
    pw = [(jnp.ones_like(abar_re), jnp.zeros_like(abar_im))]
    for _ in range(SUBLANES * L):
        pr, pi = pw[-1]
        pw.append((pr * abar_re - pi * abar_im, pr * abar_im + pi * abar_re))
    ks_rows, bs_rows, ms_cols = [], [], []
    for k in range(L):
        lr, li = pw[k]
        clr = cr * lr[:, None, :] - ci * li[:, None, :]
        cli = cr * li[:, None, :] + ci * lr[:, None, :]
        kk = jnp.einsum('gpn,gnq->gqp', clr, bb_re) - jnp.einsum('gpn,gnq->gqp', cli, bb_im)
        ks_rows.append(_block_diag(kk))
        lbr = lr[..., None] * bb_re - li[..., None] * bb_im
        lbi = lr[..., None] * bb_im + li[..., None] * bb_re
        bs_rows.append(jnp.concatenate([_block_diag(jnp.swapaxes(lbr, 1, 2)),
                                        _block_diag(jnp.swapaxes(lbi, 1, 2))], axis=1))
        lr1, li1 = pw[k + 1]
        c1r = cr * lr1[:, None, :] - ci * li1[:, None, :]
        c1i = cr * li1[:, None, :] + ci * lr1[:, None, :]
        ms_cols.append(jnp.concatenate([_block_diag(jnp.swapaxes(c1r, 1, 2)),
                                        _block_diag(-jnp.swapaxes(c1i, 1, 2))], axis=0))
    ks = jnp.concatenate(ks_rows, axis=0).astype(bf16)
    bs = jnp.concatenate(bs_rows, axis=0).astype(bf16)
    ms = jnp.concatenate(ms_cols, axis=1).astype(bf16)
    flat = lambda v: v.reshape(1, ns)
    rows = jnp.arange(SUBLANES)[:, None]
    lp = jnp.stack([jnp.where(rows >= s, flat(pw[L * s][c]), 0.0) for s in (1, 2, 4) for c in (0, 1)])
    p_re = jnp.concatenate([flat(pw[L * (r + 1)][0]) for r in range(SUBLANES)], axis=0)
    p_im = jnp.concatenate([flat(pw[L * (r + 1)][1]) for r in range(SUBLANES)], axis=0)
    wa_bd = _block_diag(rg_w_a).astype(bf16)
    wx_bd = _block_diag(rg_w_x).astype(bf16)
    row = lambda v: v.reshape(1, -1).astype(f32)
    consts = [row(g0), w_in.astype(bf16),
              rg_conv_w.astype(f32), row(rg_conv_b), wa_bd, row(rg_b_a), wx_bd, row(rg_b_x), row(rg_lambda),
              ks, bs, ms, lp, p_re, p_im, row(s5_d),
              s5_w_glu.astype(bf16), row(s5_b_glu)]
    width = w_in.shape[1]
    nc = T // L
    return pl.pallas_call(
        _even_kernel,
        grid=(bsz, nt),
        in_specs=[pl.BlockSpec((T, dm), lambda b, t: (b * nt + t, 0))]
                 + [_const_spec(c.shape) for c in consts],
        out_specs=[pl.BlockSpec((T, RG_WIDTH), lambda b, t: (b * nt + t, 0)),
                   pl.BlockSpec((T, S5_WIDTH), lambda b, t: (b * nt + t, 0))],
        out_shape=[jax.ShapeDtypeStruct((n, RG_WIDTH), f32),
                   jax.ShapeDtypeStruct((n, S5_WIDTH), f32)],
        scratch_shapes=[pltpu.VMEM((T, width), f32),
                        pltpu.VMEM((RG_WIDTH // LANES, T + SUBLANES, LANES), f32),
                        pltpu.VMEM((T, RG_WIDTH), f32), pltpu.VMEM((T, RG_WIDTH), f32),
                        pltpu.VMEM((SUBLANES, RG_WIDTH), f32),
                        pltpu.VMEM((S5_WIDTH // LANES, T + SUBLANES, LANES), f32),
                        pltpu.VMEM((nc, ns), f32), pltpu.VMEM((nc, ns), f32),
                        pltpu.VMEM((SUBLANES, ns), f32), pltpu.VMEM((SUBLANES, ns), f32),
                        pltpu.VMEM((SUBLANES + nc, 2 * ns), f32),
                        pltpu.VMEM((S5_WIDTH // LANES, T, LANES), f32)],
        compiler_params=_params(("arbitrary", "arbitrary")),
        name="even_mixer",
    )(h2d, *consts)


HG_T = 512
HG_GROUP = 4
HG_UNROLL = 2


def _split3(x):
    hi = x.astype(bf16)
    r1 = x - hi.astype(f32)
    mid = r1.astype(bf16)
    lo = (r1 - mid.astype(f32)).astype(bf16)
    return hi, mid, lo


def _hgrn_kernel(p_ref, lb_ref, nw_ref, o_ref, st_ref):
    C = HG_CHUNK
    t = pl.program_id(1)

    @pl.when(t == 0)
    def _():
        st_ref[...] = jnp.zeros_like(st_ref)

    ri = lax.broadcasted_iota(jnp.int32, (C, C), 0)
    ci = lax.broadcasted_iota(jnp.int32, (C, C), 1)
    causal = ri >= ci
    tril = causal.astype(bf16)
    lb = lb_ref[...]

    heads = range(HG_HEADS)
    sls = [slice(h * HG_DK, (h + 1) * HG_DK) for h in heads]

    def group(gidx, carry):
        prep = []
        for ci_ in range(HG_GROUP):
            r0 = pl.multiple_of((gidx * HG_GROUP + ci_) * C, C)
            q = p_ref[pl.ds(r0, C), 0:HG_W]
            z = p_ref[pl.ds(r0, C), HG_W:2 * HG_W]
            sg = jax.nn.sigmoid(z)
            log_f = jnp.log(lb + (1.0 - lb) * sg)
            kk = (1.0 - lb) * (1.0 - sg)
            hi, mid_, lo = _split3(log_f)
            cum = _dot(tril, hi) + _dot(tril, mid_) + _dot(tril, lo)
            prep.append((r0, jax.nn.silu(q), kk, cum))
        mats = []
        for r0, qs, kk, cum in prep:
            midv = cum[C // 2:C // 2 + 1, :]
            last = cum[C - 1:C, :]
            qa = (qs * jnp.exp(cum - midv)).astype(bf16)
            kb = (kk * jnp.exp(midv - cum)).astype(bf16)
            qi = (qs * jnp.exp(cum)).astype(bf16)
            kc = (kk * jnp.exp(last - cum)).astype(bf16)
            v = [p_ref[pl.ds(r0, C), 2 * HG_W + h * HG_DK:2 * HG_W + (h + 1) * HG_DK].astype(bf16)
                 for h in heads]
            sc = [_dot_nt(qa[:, sl], kb[:, sl]) for sl in sls]
            con_t = [_dot_tn(v[h], kc[:, sl]) for h, sl in enumerate(sls)]
            mats.append((r0, qi, v, sc, con_t, jnp.exp(last)))
        st = [st_ref[h] for h in heads]
        outs = []
        for r0, qi, v, sc, con_t, dec in mats:
            oi = [_dot_nt(qi[:, sl], st[h].astype(bf16)) for h, sl in enumerate(sls)]
            st = [dec[:, sl] * st[h] + con_t[h] for h, sl in enumerate(sls)]
            outs.append((r0, v, sc, oi))
        for h in heads:
            st_ref[h] = st[h]
        for r0, v, sc, oi in outs:
            o = [_dot(jnp.where(causal, sc[h], 0.0).astype(bf16), v[h]) + oi[h] for h in heads]
            for h, sl in enumerate(sls):
                g_h = p_ref[pl.ds(r0, C), 3 * HG_W + h * HG_DK:3 * HG_W + (h + 1) * HG_DK]
                on = o[h] * lax.rsqrt(jnp.mean(o[h] * o[h], axis=-1, keepdims=True) + EPS) * nw_ref[...]
                o_ref[pl.ds(r0, C), sl] = on * jax.nn.silu(g_h)
        return carry

    lax.fori_loop(0, HG_T // (C * HG_GROUP), group, 0, unroll=HG_UNROLL)


def _hgrn_mixer(hg, bsz, seq, lb, norm_w):
    n = hg.shape[0]
    T = HG_T
    nt = seq // T
    return pl.pallas_call(
        _hgrn_kernel,
        grid=(bsz, nt),
        in_specs=[pl.BlockSpec((T, 4 * HG_W), lambda b, t: (b * nt + t, 0)),
                  _const_spec((1, HG_W)), _const_spec((1, HG_DK))],
        out_specs=pl.BlockSpec((T, HG_W), lambda b, t: (b * nt + t, 0)),
        out_shape=jax.ShapeDtypeStruct((n, HG_W), f32),
        scratch_shapes=[pltpu.VMEM((HG_HEADS, HG_DK, HG_DK), f32)],
        compiler_params=_params(("arbitrary", "arbitrary")),
        name="hgrn_mixer",
    )(hg, lb.reshape(1, HG_W).astype(f32), norm_w.reshape(1, HG_DK).astype(f32))


N_PAIR = DA_WIDTH // LANES
DA_MERGE_ROWS = 256


def _attn_group(gi, d, n, sd_ref, x_ref, qd, kv, og, lg, bias_scr):
    B = DA_BLOCK
    L = DA_SPAN // d
    nblk = L // B
    nblk_log = int(math.log2(nblk))
    stride_r = L + B

    @pl.when(n == 0)
    def _():
        for r in range(d):
            kv[:, r * stride_r:r * stride_r + B, :] = jnp.zeros((2 * N_PAIR, B, LANES), bf16)

    scale = DA_HEAD_DIM ** -0.5
    for r in range(d):
        rows = pl.ds(r, L, stride=d) if d > 1 else pl.ds(0, L)
        for s in range(N_PAIR):
            qd[s, r * L:(r + 1) * L, :] = (x_ref[s, rows, :] * scale).astype(bf16)
        for s in range(2 * N_PAIR):
            kv[s, r * stride_r + B:r * stride_r + B + L, :] = x_ref[N_PAIR + s, rows, :].astype(bf16)

    qi = lax.broadcasted_iota(jnp.int32, (B, 2 * B), 0)
    kj = lax.broadcasted_iota(jnp.int32, (B, 2 * B), 1)
    rel = qi + B - kj
    valid = (rel >= 0) & (rel <= B)
    relf = rel.astype(f32)
    for h in range(DA_HEADS):
        bias = jnp.where(valid, -sd_ref[gi, h] * relf, NEG_INF)
        bias_scr[0, h] = bias
        bias_scr[1, h] = jnp.where(kj < B, NEG_INF, bias)

    lane = lax.broadcasted_iota(jnp.int32, (B, LANES), 1)
    lo_half = lane < DA_HEAD_DIM

    def blk_body(blk, carry):
        r = blk >> nblk_log
        m = blk & (nblk - 1)
        q0 = pl.multiple_of(blk * B, B)
        k0 = pl.multiple_of((blk + r) * B, B)
        first = jnp.logical_and(n == 0, m == 0).astype(jnp.int32)
        start = m * (B * d) + r
        orow = pl.ds(start, B, stride=d) if d > 1 else pl.ds(pl.multiple_of(start, B), B)
        for j in range(N_PAIR):
            qp = qd[j, pl.ds(q0, B), :]
            kp = kv[j, pl.ds(k0, 2 * B), :]
            vp = kv[N_PAIR + j, pl.ds(k0, 2 * B), :]
            outs, lses = [], []
            for e in range(2):
                keep = lo_half if e == 0 else jnp.logical_not(lo_half)
                qm = jnp.where(keep, qp, jnp.zeros_like(qp))
                s = _dot_nt(qm, kp) + bias_scr[first, 2 * j + e]
                mx = jnp.max(s, axis=-1, keepdims=True)
                ex = jnp.exp(s - mx)
                l = jnp.sum(ex, axis=-1, keepdims=True)
                outs.append(_dot(ex.astype(bf16), vp) * (1.0 / l))
                lses.append(jnp.broadcast_to(mx + jnp.log(l), (B, LANES)))
            og[gi, j, orow, :] = jnp.where(lo_half, outs[0], outs[1])
            lg[gi, j, orow, :] = jnp.where(lo_half, lses[0], lses[1])
        return carry

    lax.fori_loop(0, DA_SPAN // B, blk_body, 0, unroll=8)

    for r in range(d):
        kv[:, r * stride_r:r * stride_r + B, :] = kv[:, r * stride_r + L:r * stride_r + L + B, :]


def _attn_kernel(sd_ref, x_ref, o_ref, qd, kv0, kv1, kv2, og, lg, bias_scr):
    n = pl.program_id(1)
    g = pl.program_id(2)
    kvs = (kv0, kv1, kv2)
    for gi, (_, d) in enumerate(DA_PATTERNS):
        @pl.when(g == gi)
        def _(gi=gi, d=d):
            _attn_group(gi, d, n, sd_ref, x_ref, qd, kvs[gi], og, lg, bias_scr)

    @pl.when(g == len(DA_PATTERNS) - 1)
    def _():
        R = DA_MERGE_ROWS

        def merge(i, carry):
            r0 = pl.multiple_of(i * R, R)
            for j in range(N_PAIR):
                ls = [lg[gi, j, pl.ds(r0, R), :] for gi in range(len(DA_PATTERNS))]
                mx = functools.reduce(jnp.maximum, ls)
                ws = [jnp.exp(l - mx) for l in ls]
                den = functools.reduce(lambda a, b: a + b, ws)
                num = functools.reduce(lambda a, b: a + b,
                                       [w * og[gi, j, pl.ds(r0, R), :] for gi, w in enumerate(ws)])
                o_ref[pl.ds(r0, R), j * LANES:(j + 1) * LANES] = num * (1.0 / den)
            return carry

        lax.fori_loop(0, DA_SPAN // R, merge, 0)


def _attn_mixer(qkv_slabs, bsz, seq, sd):
    ng = len(DA_PATTERNS)
    n = qkv_slabs.shape[1]
    nt = seq // DA_SPAN
    per_g = 3 * N_PAIR
    kv_scratch = [pltpu.VMEM((2 * N_PAIR, DA_SPAN + DA_BLOCK * d, LANES), bf16) for _, d in DA_PATTERNS]
    return pl.pallas_call(
        _attn_kernel,
        grid=(bsz, nt, ng),
        in_specs=[pl.BlockSpec(memory_space=pltpu.SMEM),
                  pl.BlockSpec((per_g, DA_SPAN, LANES), lambda b, t, g: (g, b * nt + t, 0))],
        out_specs=pl.BlockSpec((DA_SPAN, DA_WIDTH), lambda b, t, g: (b * nt + t, 0)),
        out_shape=jax.ShapeDtypeStruct((n, DA_WIDTH), f32),
        scratch_shapes=[pltpu.VMEM((N_PAIR, DA_SPAN, LANES), bf16)] + kv_scratch + [
            pltpu.VMEM((ng, N_PAIR, DA_SPAN, LANES), f32),
            pltpu.VMEM((ng, N_PAIR, DA_SPAN, LANES), f32),
            pltpu.VMEM((2, DA_HEADS, DA_BLOCK, 2 * DA_BLOCK), f32)],
        compiler_params=_params(("arbitrary", "arbitrary", "arbitrary")),
        name="dilated_attn",
    )(sd, qkv_slabs)


POST_TM = 256
FF_CHUNK = 256


def _post_kernel(ya_ref, yb_ref, h_ref, woa_ref, wob_ref, g_ref, wup_ref, cw_ref, cb_ref, wdn_ref,
                 o_ref, xn_scr, st_scr, carry_scr, acc_scr, act_scr):
    tm = POST_TM
    C = FF_CHUNK
    dff = wdn_ref.shape[0]
    kc = cw_ref.shape[0]
    t = pl.program_id(1)

    @pl.when(t == 0)
    def _():
        carry_scr[...] = jnp.zeros_like(carry_scr)

    y1 = _dot(ya_ref[...].astype(bf16), woa_ref[...]) + _dot(yb_ref[...].astype(bf16), wob_ref[...])
    h1 = h_ref[...] + _rms(y1, g_ref[1:2, :])
    o_ref[...] = h1
    xn_scr[...] = _rms(h1, g_ref[2:3, :]).astype(bf16)
    acc_scr[...] = jnp.zeros_like(acc_scr)
    nchunk = dff // C

    ns = C // LANES

    def up(j):
        slot = j % 2
        xn = xn_scr[...]
        c0 = j * C
        rv = _dot(xn, wup_ref[:, c0:c0 + C])
        rg = _dot(xn, wup_ref[:, dff + c0:dff + c0 + C])
        st_scr[slot, :, 0:SUBLANES, :] = carry_scr[j]
        for s in range(ns):
            st_scr[slot, s, SUBLANES:SUBLANES + tm, :] = rv[:, s * LANES:(s + 1) * LANES]
            st_scr[slot, ns + s, SUBLANES:SUBLANES + tm, :] = rg[:, s * LANES:(s + 1) * LANES]

    def act(j):
        slot = j % 2
        parts = []
        for s in range(ns):
            c0 = j * C + s * LANES
            cv = cb_ref[:, c0:c0 + LANES]
            cg = cb_ref[:, dff + c0:dff + c0 + LANES]
            for k in range(kc):
                off = SUBLANES - (kc - 1) + k
                cv = cv + cw_ref[k:k + 1, c0:c0 + LANES] * st_scr[slot, s, off:off + tm, :]
                cg = cg + cw_ref[k:k + 1, dff + c0:dff + c0 + LANES] * st_scr[slot, ns + s, off:off + tm, :]
            parts.append((jax.nn.gelu(cg) * cv).astype(bf16))
        act_scr[slot] = jnp.concatenate(parts, axis=1)
        carry_scr[j] = st_scr[slot, :, tm:tm + SUBLANES, :]

    def down(j):
        acc_scr[...] += _dot(act_scr[j % 2], wdn_ref[j * C:(j + 1) * C, :])

    up(0)
    for j in range(nchunk):
        if j + 1 < nchunk:
            up(j + 1)
        if j >= 1:
            down(j - 1)
        act(j)
    down(nchunk - 1)
    o_ref[...] = o_ref[...] + _rms(acc_scr[...], g_ref[3:4, :])


def _post(ya, yb, h2d, bsz, seq, w_out, g, w_up, conv_w, conv_b, w_down):
    n, d = h2d.shape
    tm = POST_TM
    nt = seq // tm
    wa = ya.shape[1]
    wb = yb.shape[1]
    dff = w_down.shape[0]
    woa = w_out[:wa].astype(bf16)
    wob = w_out[wa:].astype(bf16)
    consts = [woa, wob, g.astype(f32), w_up.astype(bf16), conv_w.astype(f32),
              conv_b.reshape(1, -1).astype(f32), w_down.astype(bf16)]
    row_spec = lambda w: pl.BlockSpec((tm, w), lambda b, t: (b * nt + t, 0))
    return pl.pallas_call(
        _post_kernel,
        grid=(bsz, nt),
        in_specs=[row_spec(wa), row_spec(wb), row_spec(d)] + [_const_spec(c.shape) for c in consts],
        out_specs=row_spec(d),
        out_shape=jax.ShapeDtypeStruct((n, d), f32),
        scratch_shapes=[pltpu.VMEM((tm, d), bf16),
                        pltpu.VMEM((2, 2 * FF_CHUNK // LANES, tm + SUBLANES, LANES), f32),
                        pltpu.VMEM((dff // FF_CHUNK, 2 * FF_CHUNK // LANES, SUBLANES, LANES), f32),
                        pltpu.VMEM((tm, d), f32),
                        pltpu.VMEM((2, tm, FF_CHUNK), bf16)],
        compiler_params=_params(("arbitrary", "arbitrary")),
        name="post_ffn",
    )(ya, yb, h2d, *consts)


def kernel(x, norm_g, ffn_w_up, ffn_conv_w, ffn_conv_b, ffn_w_down, ev_w_in, ev_w_out, rg_conv_w, rg_conv_b,
           rg_w_a, rg_b_a, rg_w_x, rg_b_x, rg_lambda, s5_a_re, s5_a_im, s5_b_re, s5_b_im, s5_c_re, s5_c_im,
           s5_d, s5_log_dt, s5_w_glu, s5_b_glu, od_w_in, od_w_out, hg_lower, hg_norm_g):
    bsz, seq, d = x.shape
    depth = norm_g.shape[0]
    assert seq % DA_SPAN == 0 and d % LANES == 0
    lb_p = jax.nn.softmax(hg_lower.astype(f32), axis=0)
    lb_all = jnp.cumsum(lb_p, axis=0) - lb_p[0]
    ngrp = len(DA_PATTERNS)
    slopes = (2.0 ** (-8.0 * jnp.arange(1, ngrp * DA_HEADS + 1, dtype=f32) / (ngrp * DA_HEADS))
              ).reshape(ngrp, DA_HEADS)
    sd = slopes * jnp.asarray([float(p[1]) for p in DA_PATTERNS], f32)[:, None]

    h = x.reshape(bsz * seq, d)
    for layer in range(depth):
        j = layer // 2
        g = norm_g[layer]
        if layer % 2 == 0:
            ya, yb = _even_mixer(h, g[0], ev_w_in[j], bsz, seq, rg_conv_w[j], rg_conv_b[j], rg_w_a[j], rg_b_a[j],
                                 rg_w_x[j], rg_b_x[j], rg_lambda[j], s5_a_re[j], s5_a_im[j],
                                 s5_b_re[j], s5_b_im[j], s5_c_re[j], s5_c_im[j], s5_d[j], s5_log_dt[j],
                                 s5_w_glu[j], s5_b_glu[j])
            w_out = ev_w_out[j]
        else:
            hgp, qkv = _norm_proj(h, g[0], od_w_in[j].astype(bf16),
                                  (("flat", 4 * HG_W), ("slab", ngrp * 3 * DA_WIDTH)))
            ya = _hgrn_mixer(hgp, bsz, seq, lb_all[layer], hg_norm_g[j])
            yb = _attn_mixer(qkv, bsz, seq, sd)
            w_out = od_w_out[j]
        h = _post(ya, yb, h, bsz, seq, w_out, g, ffn_w_up[layer], ffn_conv_w[layer], ffn_conv_b[layer],
                  ffn_w_down[layer])
    return h.reshape(bsz, seq, d)
```

```python
import functools
import math

import jax
import jax.numpy as jnp
from jax import lax
from jax.experimental import pallas as pl
from jax.experimental.pallas import tpu as pltpu

f32 = jnp.float32
bf16 = jnp.bfloat16

EPS = 1e-6
NEG_INF = -1e30
RG_C = 8.0
RG_HEADS = 8
RG_WIDTH = 512
S5_WIDTH = 256
S5_GROUP = 16
S5_STATE = 64
HG_HEADS = 4
HG_DK = 128
HG_CHUNK = 64
HG_W = HG_HEADS * HG_DK
DA_HEADS = 4
DA_HEAD_DIM = 64
DA_WIDTH = DA_HEADS * DA_HEAD_DIM
DA_PATTERNS = ((128, 1), (512, 4), (2048, 16))
DA_BLOCK = 128
DA_SPAN = 2048
LANES = 128
SUBLANES = 8
VMEM_LIMIT = 56 * 1024 * 1024


def _rms(x, g):
    return x * lax.rsqrt(jnp.mean(x * x, axis=-1, keepdims=True) + EPS) * g


def _dot(a, b):
    return jnp.dot(a, b, preferred_element_type=f32)


def _dot_nt(a, b):
    return lax.dot_general(a, b, (((1,), (1,)), ((), ())), preferred_element_type=f32)


def _dot_tn(a, b):
    return lax.dot_general(a, b, (((0,), (0,)), ((), ())), preferred_element_type=f32)


def _const_spec(shape):
    nd = len(shape)
    return pl.BlockSpec(shape, lambda *_: (0,) * nd, pipeline_mode=pl.Buffered(1))


def _params(sem):
    return pltpu.CompilerParams(dimension_semantics=sem, vmem_limit_bytes=VMEM_LIMIT)


PROJ_TM = 512
PROJ_CHUNK = 512


def _norm_proj_kernel(x_ref, g_ref, w_ref, *out_refs, layout):
    xn = _rms(x_ref[...], g_ref[...]).astype(bf16)
    col = 0
    for o_ref, (kind, width) in zip(out_refs, layout):
        for c0 in range(0, width, PROJ_CHUNK):
            cw = min(PROJ_CHUNK, width - c0)
            res = _dot(xn, w_ref[:, col + c0:col + c0 + cw])
            if kind == "flat":
                o_ref[:, c0:c0 + cw] = res
            else:
                for s in range(cw // LANES):
                    o_ref[c0 // LANES + s] = res[:, s * LANES:(s + 1) * LANES]
        col += width


def _norm_proj(h2d, g, w, layout):
    n, d = h2d.shape
    tm = PROJ_TM
    out_shape, out_specs = [], []
    for kind, width in layout:
        if kind == "flat":
            out_shape.append(jax.ShapeDtypeStruct((n, width), f32))
            out_specs.append(pl.BlockSpec((tm, width), lambda i: (i, 0)))
        else:
            out_shape.append(jax.ShapeDtypeStruct((width // LANES, n, LANES), f32))
            out_specs.append(pl.BlockSpec((width // LANES, tm, LANES), lambda i: (0, i, 0)))
    return pl.pallas_call(
        functools.partial(_norm_proj_kernel, layout=layout),
        grid=(n // tm,),
        in_specs=[pl.BlockSpec((tm, d), lambda i: (i, 0)),
                  _const_spec((1, d)),
                  _const_spec(w.shape)],
        out_specs=out_specs,
        out_shape=out_shape,
        compiler_params=_params(("arbitrary",)),
        name="norm_proj",
    )(h2d, g.reshape(1, d), w)


EV_T = 512
S5_L = 4
EV_PROJ_CHUNK = 256


def _even_kernel(h_ref, g_ref, win_ref, rcw_ref, rcb_ref, wa_ref, ba_ref, wx_ref, bx_ref, lam_ref,
                 ks_ref, bs_ref, ms_ref, lp_ref, pre_ref, pim_ref,
                 d5_ref, wg_ref, bg_ref,
                 ya_ref, yb_ref,
                 pj, xst, rga, rgb, rgc, ust, sre, sim, scr, sci, hst, yst):
    T = EV_T
    t = pl.program_id(1)

    @pl.when(t == 0)
    def _():
        xst[:, 0:SUBLANES, :] = jnp.zeros((RG_WIDTH // LANES, SUBLANES, LANES), f32)
        ust[:, 0:SUBLANES, :] = jnp.zeros((S5_WIDTH // LANES, SUBLANES, LANES), f32)
        rgc[...] = jnp.zeros_like(rgc)
        scr[...] = jnp.zeros_like(scr)
        sci[...] = jnp.zeros_like(sci)

    xn = _rms(h_ref[...], g_ref[...]).astype(bf16)
    for c0 in range(0, pj.shape[1], EV_PROJ_CHUNK):
        pj[:, c0:c0 + EV_PROJ_CHUNK] = _dot(xn, win_ref[:, c0:c0 + EV_PROJ_CHUNK])
    p_ref = pj

    kc = rcw_ref.shape[0]
    us = []
    for s in range(RG_WIDTH // LANES):
        cols = slice(s * LANES, (s + 1) * LANES)
        xst[s, SUBLANES:SUBLANES + T, :] = p_ref[:, cols]
        us_ = rcb_ref[:, cols]
        for k in range(kc):
            off = SUBLANES - (kc - 1) + k
            us_ = us_ + rcw_ref[k:k + 1, cols] * xst[s, off:off + T, :]
        us.append(us_)
    u = jnp.concatenate(us, axis=1)
    xst[:, 0:SUBLANES, :] = xst[:, T:T + SUBLANES, :]
    ub = u.astype(bf16)
    r = jax.nn.sigmoid(_dot(ub, wa_ref[...]) + ba_ref[...])
    ig = jax.nn.sigmoid(_dot(ub, wx_ref[...]) + bx_ref[...])
    log_a = (RG_C * r) * jax.nn.log_sigmoid(lam_ref[...])
    a = jnp.exp(log_a)
    bt = jnp.sqrt(-jnp.tanh(log_a) * (a * a + 1.0)) * (ig * u)
    ngrp = T // SUBLANES
    a = a.reshape(ngrp, SUBLANES, RG_WIDTH)
    bt = bt.reshape(ngrp, SUBLANES, RG_WIDTH)
    rowm = lax.broadcasted_iota(jnp.int32, (1, SUBLANES, RG_WIDTH), 1)
    for s in (1, 2, 4):
        m = rowm >= s
        bt = bt + jnp.where(m, a * pltpu.roll(bt, s, 1), 0.0)
        a = a * jnp.where(m, pltpu.roll(a, s, 1), 1.0)
    rga[...] = a.reshape(T, RG_WIDTH)
    rgb[...] = bt.reshape(T, RG_WIDTH)
    ya_ref[...] = jax.nn.gelu(p_ref[:, RG_WIDTH:2 * RG_WIDTH])

    L = S5_L
    nc = T // L
    ns = sre.shape[1]
    nsl = S5_WIDTH // LANES
    u5 = p_ref[:, 2 * RG_WIDTH:2 * RG_WIDTH + S5_WIDTH]
    for s in range(nsl):
        ust[s, SUBLANES:SUBLANES + T, :] = u5[:, s * LANES:(s + 1) * LANES]
    rowl = lax.broadcasted_iota(jnp.int32, (T, LANES), 0) & (L - 1)
    pieces, ends = [], []
    for k in range(L):
        for s in range(nsl):
            p = ust[s, SUBLANES - k:SUBLANES - k + T, :]
            if k:
                p = jnp.where(rowl >= k, p, 0.0)
            pieces.append(p.astype(bf16))
            ends.append(ust[s, pl.ds(SUBLANES + L - 1 - k, nc, stride=L), :].astype(bf16))
    yb_ref[...] = _dot(jnp.concatenate(pieces, axis=1), ks_ref[...]) + d5_ref[...] * u5
    gend = _dot(jnp.concatenate(ends, axis=1), bs_ref[...])
    ngrp5 = nc // SUBLANES
    hre = gend[:, 0:ns].reshape(ngrp5, SUBLANES, ns)
    him = gend[:, ns:2 * ns].reshape(ngrp5, SUBLANES, ns)
    for i, s in enumerate((1, 2, 4)):
        cr = lp_ref[2 * i][None]
        ci = lp_ref[2 * i + 1][None]
        zr = pltpu.roll(hre, s, 1)
        zi = pltpu.roll(him, s, 1)
        hre, him = hre + (cr * zr - ci * zi), him + (cr * zi + ci * zr)
    sre[...] = hre.reshape(nc, ns)
    sim[...] = him.reshape(nc, ns)
    hst[0:SUBLANES, 0:ns] = scr[...]
    hst[0:SUBLANES, ns:2 * ns] = sci[...]

    def rg_body(gidx, c):
        r0 = pl.multiple_of(gidx * SUBLANES, SUBLANES)
        h = rga[pl.ds(r0, SUBLANES), :] * c + rgb[pl.ds(r0, SUBLANES), :]
        rgb[pl.ds(r0, SUBLANES), :] = h
        return jnp.broadcast_to(h[SUBLANES - 1:SUBLANES, :], (SUBLANES, RG_WIDTH))

    rgc[...] = lax.fori_loop(0, T // SUBLANES, rg_body, rgc[...], unroll=4)

    def s5_body(gidx, c):
        cre, cim = c
        r0 = pl.multiple_of(gidx * SUBLANES, SUBLANES)
        pr = pre_ref[...]
        pi = pim_ref[...]
        hr = sre[pl.ds(r0, SUBLANES), :] + (pr * cre - pi * cim)
        hi = sim[pl.ds(r0, SUBLANES), :] + (pr * cim + pi * cre)
        sre[pl.ds(r0, SUBLANES), :] = hr
        sim[pl.ds(r0, SUBLANES), :] = hi
        return (jnp.broadcast_to(hr[SUBLANES - 1:SUBLANES, :], (SUBLANES, ns)),
                jnp.broadcast_to(hi[SUBLANES - 1:SUBLANES, :], (SUBLANES, ns)))

    cre, cim = lax.fori_loop(0, ngrp5, s5_body, (scr[...], sci[...]), unroll=2)
    scr[...] = cre
    sci[...] = cim
    hst[SUBLANES:SUBLANES + nc, 0:ns] = sre[...]
    hst[SUBLANES:SUBLANES + nc, ns:2 * ns] = sim[...]
    hprev = hst[SUBLANES - 1:SUBLANES - 1 + nc, :].astype(bf16)
    yi = _dot(hprev, ms_ref[...])
    for r in range(L):
        for s in range(nsl):
            c0 = r * S5_WIDTH + s * LANES
            yst[s, pl.ds(r, nc, stride=L), :] = yi[:, c0:c0 + LANES]
    ya_ref[...] = ya_ref[...] * rgb[...]
    y5 = yb_ref[...] + jnp.concatenate([yst[s] for s in range(nsl)], axis=1)
    v = jax.nn.gelu(y5)
    yb_ref[...] = v * jax.nn.sigmoid(_dot(v.astype(bf16), wg_ref[...]) + bg_ref[...])


def _block_diag(w):
    *lead, g, a, b = w.shape
    eye = jnp.eye(g, dtype=w.dtype)
    return (eye[:, None, :, None] * w[..., :, :, None, :]).reshape(*lead, g * a, g * b)


def _even_mixer(h2d, g0, w_in, bsz, seq, rg_conv_w, rg_conv_b, rg_w_a, rg_b_a, rg_w_x, rg_b_x, rg_lambda,
                s5_a_re, s5_a_im, s5_b_re, s5_b_im, s5_c_re, s5_c_im, s5_d, s5_log_dt,
                s5_w_glu, s5_b_glu):
    n, dm = h2d.shape
    T = EV_T
    L = S5_L
    nt = seq // T
    ns = s5_a_re.shape[0] * S5_STATE
    dt = jnp.exp(s5_log_dt.astype(f32))[:, None]
    ar = s5_a_re.astype(f32)
    ai = s5_a_im.astype(f32)
    mag = jnp.exp(ar * dt)
    abar_re = mag * jnp.cos(ai * dt)
    abar_im = mag * jnp.sin(ai * dt)
    den = ar * ar + ai * ai
    num_re = abar_re - 1.0
    f_re = (num_re * ar + abar_im * ai) / den
    f_im = (abar_im * ar - num_re * ai) / den
    br = s5_b_re.astype(f32)
    bi = s5_b_im.astype(f32)
    bb_re = f_re[..., None] * br - f_im[..., None] * bi
    bb_im = f_re[..., None] * bi + f_im[..., None] * br
    cr = s5_c_re.astype(f32)
    ci = s5_c_im.astype(f32)
    jj = jnp.arange(SUBLANES * L + 1, dtype=f32)[:, None, None]
    pmag = jnp.exp(jj * (ar * dt)[None])
    pang = jj * (ai * dt)[None]
    pwr = pmag * jnp.cos(pang)
    pwi = pmag * jnp.sin(pang)
    lr, li = pwr[:L, :, None, :], pwi[:L, :, None, :]
    clr = cr[None] * lr - ci[None] * li
    cli = cr[None] * li + ci[None] * lr
    kk = jnp.einsum('kgpn,gnq->kgqp', clr, bb_re) - jnp.einsum('kgpn,gnq->kgqp', cli, bb_im)
    ks = _block_diag(kk).reshape(L * S5_WIDTH, S5_WIDTH).astype(bf16)
    lbr = pwr[:L, :, :, None] * bb_re[None] - pwi[:L, :, :, None] * bb_im[None]
    lbi = pwr[:L, :, :, None] * bb_im[None] + pwi[:L, :, :, None] * bb_re[None]
    bs = jnp.concatenate([_block_diag(jnp.swapaxes(lbr, 2, 3)), _block_diag(jnp.swapaxes(lbi, 2, 3))],
                         axis=2).reshape(L * S5_WIDTH, 2 * ns).astype(bf16)
    l1r, l1i = pwr[1:L + 1, :, None, :], pwi[1:L + 1, :, None, :]
    c1r = cr[None] * l1r - ci[None] * l1i
    c1i = cr[None] * l1i + ci[None] * l1r
    ms = jnp.concatenate([_block_diag(jnp.swapaxes(c1r, 2, 3)), _block_diag(-jnp.swapaxes(c1i, 2, 3))], axis=1)
    ms = jnp.swapaxes(ms, 0, 1).reshape(2 * ns, L * S5_WIDTH).astype(bf16)
    shifts = jnp.asarray((1, 2, 4))
    rows = jnp.arange(SUBLANES)[None, :, None]
    keep = rows >= shifts[:, None, None]
    lp = jnp.stack([jnp.where(keep, p[L * shifts].reshape(3, 1, ns), 0.0) for p in (pwr, pwi)],
                   axis=1).reshape(6, SUBLANES, ns)
    p_re = pwr[L::L].reshape(SUBLANES, ns)
    p_im = pwi[L::L].reshape(SUBLANES, ns)
    wa_bd = _block_diag(rg_w_a).astype(bf16)
    wx_bd = _block_diag(rg_w_x).astype(bf16)
    row = lambda v: v.reshape(1, -1).astype(f32)
    consts = [row(g0), w_in.astype(bf16),
              rg_conv_w.astype(f32), row(rg_conv_b), wa_bd, row(rg_b_a), wx_bd, row(rg_b_x), row(rg_lambda),
              ks, bs, ms, lp, p_re, p_im, row(s5_d),
              s5_w_glu.astype(bf16), row(s5_b_glu)]
    width = w_in.shape[1]
    nc = T // L
    return pl.pallas_call(
        _even_kernel,
        grid=(bsz, nt),
        in_specs=[pl.BlockSpec((T, dm), lambda b, t: (b * nt + t, 0))]
                 + [_const_spec(c.shape) for c in consts],
        out_specs=[pl.BlockSpec((T, RG_WIDTH), lambda b, t: (b * nt + t, 0)),
                   pl.BlockSpec((T, S5_WIDTH), lambda b, t: (b * nt + t, 0))],
        out_shape=[jax.ShapeDtypeStruct((n, RG_WIDTH), f32),
                   jax.ShapeDtypeStruct((n, S5_WIDTH), f32)],
        scratch_shapes=[pltpu.VMEM((T, width), f32),
                        pltpu.VMEM((RG_WIDTH // LANES, T + SUBLANES, LANES), f32),
                        pltpu.VMEM((T, RG_WIDTH), f32), pltpu.VMEM((T, RG_WIDTH), f32),
                        pltpu.VMEM((SUBLANES, RG_WIDTH), f32),
                        pltpu.VMEM((S5_WIDTH // LANES, T + SUBLANES, LANES), f32),
                        pltpu.VMEM((nc, ns), f32), pltpu.VMEM((nc, ns), f32),
                        pltpu.VMEM((SUBLANES, ns), f32), pltpu.VMEM((SUBLANES, ns), f32),
                        pltpu.VMEM((SUBLANES + nc, 2 * ns), f32),
                        pltpu.VMEM((S5_WIDTH // LANES, T, LANES), f32)],
        compiler_params=_params(("arbitrary", "arbitrary")),
        name="even_mixer",
    )(h2d, *consts)


HG_T = 512
HG_GROUP = 4
HG_UNROLL = 2


def _split3(x):
    hi = x.astype(bf16)
    r1 = x - hi.astype(f32)
    mid = r1.astype(bf16)
    lo = (r1 - mid.astype(f32)).astype(bf16)
    return hi, mid, lo


def _hgrn_kernel(p_ref, lb_ref, nw_ref, o_ref, st_ref):
    C = HG_CHUNK
    t = pl.program_id(1)

    @pl.when(t == 0)
    def _():
        st_ref[...] = jnp.zeros_like(st_ref)

    ri = lax.broadcasted_iota(jnp.int32, (C, C), 0)
    ci = lax.broadcasted_iota(jnp.int32, (C, C), 1)
    causal = ri >= ci
    tril = causal.astype(bf16)
    lb = lb_ref[...]

    heads = range(HG_HEADS)
    sls = [slice(h * HG_DK, (h + 1) * HG_DK) for h in heads]

    def group(gidx, carry):
        prep = []
        for ci_ in range(HG_GROUP):
            r0 = pl.multiple_of((gidx * HG_GROUP + ci_) * C, C)
            q = p_ref[pl.ds(r0, C), 0:HG_W]
            z = p_ref[pl.ds(r0, C), HG_W:2 * HG_W]
            sg = jax.nn.sigmoid(z)
            log_f = jnp.log(lb + (1.0 - lb) * sg)
            kk = (1.0 - lb) * (1.0 - sg)
            hi, mid_, lo = _split3(log_f)
            cum = _dot(tril, hi) + _dot(tril, mid_) + _dot(tril, lo)
            prep.append((r0, jax.nn.silu(q), kk, cum))
        mats = []
        for r0, qs, kk, cum in prep:
            midv = cum[C // 2:C // 2 + 1, :]
            last = cum[C - 1:C, :]
            qa = (qs * jnp.exp(cum - midv)).astype(bf16)
            kb = (kk * jnp.exp(midv - cum)).astype(bf16)
            qi = (qs * jnp.exp(cum)).astype(bf16)
            kc = (kk * jnp.exp(last - cum)).astype(bf16)
            v = [p_ref[pl.ds(r0, C), 2 * HG_W + h * HG_DK:2 * HG_W + (h + 1) * HG_DK].astype(bf16)
                 for h in heads]
            sc = [_dot_nt(qa[:, sl], kb[:, sl]) for sl in sls]
            con_t = [_dot_tn(v[h], kc[:, sl]) for h, sl in enumerate(sls)]
            mats.append((r0, qi, v, sc, con_t, jnp.exp(last)))
        st = [st_ref[h] for h in heads]
        outs = []
        for r0, qi, v, sc, con_t, dec in mats:
            oi = [_dot_nt(qi[:, sl], st[h].astype(bf16)) for h, sl in enumerate(sls)]
            st = [dec[:, sl] * st[h] + con_t[h] for h, sl in enumerate(sls)]
            outs.append((r0, v, sc, oi))
        for h in heads:
            st_ref[h] = st[h]
        for r0, v, sc, oi in outs:
            o = [_dot(jnp.where(causal, sc[h], 0.0).astype(bf16), v[h]) + oi[h] for h in heads]
            for h, sl in enumerate(sls):
                g_h = p_ref[pl.ds(r0, C), 3 * HG_W + h * HG_DK:3 * HG_W + (h + 1) * HG_DK]
                on = o[h] * lax.rsqrt(jnp.mean(o[h] * o[h], axis=-1, keepdims=True) + EPS) * nw_ref[...]
                o_ref[pl.ds(r0, C), sl] = on * jax.nn.silu(g_h)
        return carry

    lax.fori_loop(0, HG_T // (C * HG_GROUP), group, 0, unroll=HG_UNROLL)


def _hgrn_mixer(hg, bsz, seq, lb, norm_w):
    n = hg.shape[0]
    T = HG_T
    nt = seq // T
    return pl.pallas_call(
        _hgrn_kernel,
        grid=(bsz, nt),
        in_specs=[pl.BlockSpec((T, 4 * HG_W), lambda b, t: (b * nt + t, 0)),
                  _const_spec((1, HG_W)), _const_spec((1, HG_DK))],
        out_specs=pl.BlockSpec((T, HG_W), lambda b, t: (b * nt + t, 0)),
        out_shape=jax.ShapeDtypeStruct((n, HG_W), f32),
        scratch_shapes=[pltpu.VMEM((HG_HEADS, HG_DK, HG_DK), f32)],
        compiler_params=_params(("arbitrary", "arbitrary")),
        name="hgrn_mixer",
    )(hg, lb.reshape(1, HG_W).astype(f32), norm_w.reshape(1, HG_DK).astype(f32))


N_PAIR = DA_WIDTH // LANES
DA_MERGE_ROWS = 256


def _attn_group(gi, d, n, sd_ref, x_ref, qd, kv, og, lg, bias_scr):
    B = DA_BLOCK
    L = DA_SPAN // d
    nblk = L // B
    nblk_log = int(math.log2(nblk))
    stride_r = L + B

    @pl.when(n == 0)
    def _():
        for r in range(d):
            kv[:, r * stride_r:r * stride_r + B, :] = jnp.zeros((2 * N_PAIR, B, LANES), bf16)

    scale = DA_HEAD_DIM ** -0.5
    for r in range(d):
        rows = pl.ds(r, L, stride=d) if d > 1 else pl.ds(0, L)
        for s in range(N_PAIR):
            qd[s, r * L:(r + 1) * L, :] = (x_ref[s, rows, :] * scale).astype(bf16)
        for s in range(2 * N_PAIR):
            kv[s, r * stride_r + B:r * stride_r + B + L, :] = x_ref[N_PAIR + s, rows, :].astype(bf16)

    qi = lax.broadcasted_iota(jnp.int32, (B, 2 * B), 0)
    kj = lax.broadcasted_iota(jnp.int32, (B, 2 * B), 1)
    rel = qi + B - kj
    valid = (rel >= 0) & (rel <= B)
    relf = rel.astype(f32)
    for h in range(DA_HEADS):
        bias = jnp.where(valid, -sd_ref[gi, h] * relf, NEG_INF)
        bias_scr[0, h] = bias
        bias_scr[1, h] = jnp.where(kj < B, NEG_INF, bias)

    lane = lax.broadcasted_iota(jnp.int32, (B, LANES), 1)
    lo_half = lane < DA_HEAD_DIM

    def blk_body(blk, carry):
        r = blk >> nblk_log
        m = blk & (nblk - 1)
        q0 = pl.multiple_of(blk * B, B)
        k0 = pl.multiple_of((blk + r) * B, B)
        first = jnp.logical_and(n == 0, m == 0).astype(jnp.int32)
        start = m * (B * d) + r
        orow = pl.ds(start, B, stride=d) if d > 1 else pl.ds(pl.multiple_of(start, B), B)
        for j in range(N_PAIR):
            qp = qd[j, pl.ds(q0, B), :]
            kp = kv[j, pl.ds(k0, 2 * B), :]
            vp = kv[N_PAIR + j, pl.ds(k0, 2 * B), :]
            outs, lses = [], []
            for e in range(2):
                keep = lo_half if e == 0 else jnp.logical_not(lo_half)
                qm = jnp.where(keep, qp, jnp.zeros_like(qp))
                s = _dot_nt(qm, kp) + bias_scr[first, 2 * j + e]
                mx = jnp.max(s, axis=-1, keepdims=True)
                ex = jnp.exp(s - mx)
                l = jnp.sum(ex, axis=-1, keepdims=True)
                outs.append(_dot(ex.astype(bf16), vp) * (1.0 / l))
                lses.append(jnp.broadcast_to(mx + jnp.log(l), (B, LANES)))
            og[gi, j, orow, :] = jnp.where(lo_half, outs[0], outs[1])
            lg[gi, j, orow, :] = jnp.where(lo_half, lses[0], lses[1])
        return carry

    lax.fori_loop(0, DA_SPAN // B, blk_body, 0, unroll=8)

    for r in range(d):
        kv[:, r * stride_r:r * stride_r + B, :] = kv[:, r * stride_r + L:r * stride_r + L + B, :]


def _attn_kernel(sd_ref, x_ref, o_ref, qd, kv0, kv1, kv2, og, lg, bias_scr):
    n = pl.program_id(1)
    g = pl.program_id(2)
    kvs = (kv0, kv1, kv2)
    for gi, (_, d) in enumerate(DA_PATTERNS):
        @pl.when(g == gi)
        def _(gi=gi, d=d):
            _attn_group(gi, d, n, sd_ref, x_ref, qd, kvs[gi], og, lg, bias_scr)

    @pl.when(g == len(DA_PATTERNS) - 1)
    def _():
        R = DA_MERGE_ROWS

        def merge(i, carry):
            r0 = pl.multiple_of(i * R, R)
            for j in range(N_PAIR):
                ls = [lg[gi, j, pl.ds(r0, R), :] for gi in range(len(DA_PATTERNS))]
                mx = functools.reduce(jnp.maximum, ls)
                ws = [jnp.exp(l - mx) for l in ls]
                den = functools.reduce(lambda a, b: a + b, ws)
                num = functools.reduce(lambda a, b: a + b,
                                       [w * og[gi, j, pl.ds(r0, R), :] for gi, w in enumerate(ws)])
                o_ref[pl.ds(r0, R), j * LANES:(j + 1) * LANES] = num * (1.0 / den)
            return carry

        lax.fori_loop(0, DA_SPAN // R, merge, 0)


def _attn_mixer(qkv_slabs, bsz, seq, sd):
    ng = len(DA_PATTERNS)
    n = qkv_slabs.shape[1]
    nt = seq // DA_SPAN
    per_g = 3 * N_PAIR
    kv_scratch = [pltpu.VMEM((2 * N_PAIR, DA_SPAN + DA_BLOCK * d, LANES), bf16) for _, d in DA_PATTERNS]
    return pl.pallas_call(
        _attn_kernel,
        grid=(bsz, nt, ng),
        in_specs=[pl.BlockSpec(memory_space=pltpu.SMEM),
                  pl.BlockSpec((per_g, DA_SPAN, LANES), lambda b, t, g: (g, b * nt + t, 0))],
        out_specs=pl.BlockSpec((DA_SPAN, DA_WIDTH), lambda b, t, g: (b * nt + t, 0)),
        out_shape=jax.ShapeDtypeStruct((n, DA_WIDTH), f32),
        scratch_shapes=[pltpu.VMEM((N_PAIR, DA_SPAN, LANES), bf16)] + kv_scratch + [
            pltpu.VMEM((ng, N_PAIR, DA_SPAN, LANES), f32),
            pltpu.VMEM((ng, N_PAIR, DA_SPAN, LANES), f32),
            pltpu.VMEM((2, DA_HEADS, DA_BLOCK, 2 * DA_BLOCK), f32)],
        compiler_params=_params(("arbitrary", "arbitrary", "arbitrary")),
        name="dilated_attn",
    )(sd, qkv_slabs)


POST_TM = 256
FF_CHUNK = 256


def _post_kernel(ya_ref, yb_ref, h_ref, woa_ref, wob_ref, g_ref, wup_ref, cw_ref, cb_ref, wdn_ref,
                 o_ref, xn_scr, st_scr, carry_scr, acc_scr, act_scr):
    tm = POST_TM
    C = FF_CHUNK
    dff = wdn_ref.shape[0]
    kc = cw_ref.shape[0]
    t = pl.program_id(1)

    @pl.when(t == 0)
    def _():
        carry_scr[...] = jnp.zeros_like(carry_scr)

    y1 = _dot(ya_ref[...].astype(bf16), woa_ref[...]) + _dot(yb_ref[...].astype(bf16), wob_ref[...])
    h1 = h_ref[...] + _rms(y1, g_ref[1:2, :])
    o_ref[...] = h1
    xn_scr[...] = _rms(h1, g_ref[2:3, :]).astype(bf16)
    acc_scr[...] = jnp.zeros_like(acc_scr)
    nchunk = dff // C

    ns = C // LANES

    def up(j):
        slot = j % 2
        xn = xn_scr[...]
        c0 = j * C
        rv = _dot(xn, wup_ref[:, c0:c0 + C])
        rg = _dot(xn, wup_ref[:, dff + c0:dff + c0 + C])
        st_scr[slot, :, 0:SUBLANES, :] = carry_scr[j]
        for s in range(ns):
            st_scr[slot, s, SUBLANES:SUBLANES + tm, :] = rv[:, s * LANES:(s + 1) * LANES]
            st_scr[slot, ns + s, SUBLANES:SUBLANES + tm, :] = rg[:, s * LANES:(s + 1) * LANES]

    def act(j):
        slot = j % 2
        parts = []
        for s in range(ns):
            c0 = j * C + s * LANES
            cv = cb_ref[:, c0:c0 + LANES]
            cg = cb_ref[:, dff + c0:dff + c0 + LANES]
            for k in range(kc):
                off = SUBLANES - (kc - 1) + k
                cv = cv + cw_ref[k:k + 1, c0:c0 + LANES] * st_scr[slot, s, off:off + tm, :]
                cg = cg + cw_ref[k:k + 1, dff + c0:dff + c0 + LANES] * st_scr[slot, ns + s, off:off + tm, :]
            parts.append((jax.nn.gelu(cg) * cv).astype(bf16))
        act_scr[slot] = jnp.concatenate(parts, axis=1)
        carry_scr[j] = st_scr[slot, :, tm:tm + SUBLANES, :]

    def down(j):
        acc_scr[...] += _dot(act_scr[j % 2], wdn_ref[j * C:(j + 1) * C, :])

    up(0)
    for j in range(nchunk):
        if j + 1 < nchunk:
            up(j + 1)
        if j >= 1:
            down(j - 1)
        act(j)
    down(nchunk - 1)
    o_ref[...] = o_ref[...] + _rms(acc_scr[...], g_ref[3:4, :])


def _layer_spec(shape, layer):
    nd = len(shape) - 1
    return pl.BlockSpec((None,) + tuple(shape[1:]), lambda *_: (layer,) + (0,) * nd,
                        pipeline_mode=pl.Buffered(1))


def _post(ya, yb, h2d, bsz, seq, w_out, layer, g_all, w_up_all, conv_w_all, conv_b_all, w_down_all):
    n, d = h2d.shape
    tm = POST_TM
    nt = seq // tm
    wa = ya.shape[1]
    wb = yb.shape[1]
    dff = w_down_all.shape[1]
    woa = w_out[:wa].astype(bf16)
    wob = w_out[wa:].astype(bf16)
    stacked = [g_all, w_up_all, conv_w_all, conv_b_all, w_down_all]
    consts = [woa, wob] + stacked
    row_spec = lambda w: pl.BlockSpec((tm, w), lambda b, t: (b * nt + t, 0))
    return pl.pallas_call(
        _post_kernel,
        grid=(bsz, nt),
        in_specs=[row_spec(wa), row_spec(wb), row_spec(d), _const_spec(woa.shape), _const_spec(wob.shape)]
                 + [_layer_spec(c.shape, layer) for c in stacked],
        out_specs=row_spec(d),
        out_shape=jax.ShapeDtypeStruct((n, d), f32),
        scratch_shapes=[pltpu.VMEM((tm, d), bf16),
                        pltpu.VMEM((2, 2 * FF_CHUNK // LANES, tm + SUBLANES, LANES), f32),
                        pltpu.VMEM((dff // FF_CHUNK, 2 * FF_CHUNK // LANES, SUBLANES, LANES), f32),
                        pltpu.VMEM((tm, d), f32),
                        pltpu.VMEM((2, tm, FF_CHUNK), bf16)],
        compiler_params=_params(("arbitrary", "arbitrary")),
        name="post_ffn",
    )(ya, yb, h2d, *consts)


def kernel(x, norm_g, ffn_w_up, ffn_conv_w, ffn_conv_b, ffn_w_down, ev_w_in, ev_w_out, rg_conv_w, rg_conv_b,
           rg_w_a, rg_b_a, rg_w_x, rg_b_x, rg_lambda, s5_a_re, s5_a_im, s5_b_re, s5_b_im, s5_c_re, s5_c_im,
           s5_d, s5_log_dt, s5_w_glu, s5_b_glu, od_w_in, od_w_out, hg_lower, hg_norm_g):
    bsz, seq, d = x.shape
    depth = norm_g.shape[0]
    assert seq % DA_SPAN == 0 and d % LANES == 0
    lb_p = jax.nn.softmax(hg_lower.astype(f32), axis=0)
    lb_all = jnp.cumsum(lb_p, axis=0) - lb_p[0]
    ngrp = len(DA_PATTERNS)
    slopes = (2.0 ** (-8.0 * jnp.arange(1, ngrp * DA_HEADS + 1, dtype=f32) / (ngrp * DA_HEADS))
              ).reshape(ngrp, DA_HEADS)
    sd = slopes * jnp.asarray([float(p[1]) for p in DA_PATTERNS], f32)[:, None]

    g_all = norm_g.astype(f32)
    w_up_all = ffn_w_up.astype(bf16)
    w_down_all = ffn_w_down.astype(bf16)
    conv_w_all = ffn_conv_w.astype(f32)
    conv_b_all = ffn_conv_b.astype(f32).reshape(depth, 1, -1)

    h = x.reshape(bsz * seq, d)
    for layer in range(depth):
        j = layer // 2
        g = norm_g[layer]
        if layer % 2 == 0:
            ya, yb = _even_mixer(h, g[0], ev_w_in[j], bsz, seq, rg_conv_w[j], rg_conv_b[j], rg_w_a[j], rg_b_a[j],
                                 rg_w_x[j], rg_b_x[j], rg_lambda[j], s5_a_re[j], s5_a_im[j],
                                 s5_b_re[j], s5_b_im[j], s5_c_re[j], s5_c_im[j], s5_d[j], s5_log_dt[j],
                                 s5_w_glu[j], s5_b_glu[j])
            w_out = ev_w_out[j]
        else:
            hgp, qkv = _norm_proj(h, g[0], od_w_in[j].astype(bf16),
                                  (("flat", 4 * HG_W), ("slab", ngrp * 3 * DA_WIDTH)))
            ya = _hgrn_mixer(hgp, bsz, seq, lb_all[layer], hg_norm_g[j])
            yb = _attn_mixer(qkv, bsz, seq, sd)
            w_out = od_w_out[j]
        h = _post(ya, yb, h, bsz, seq, w_out, layer, g_all, w_up_all, conv_w_all, conv_b_all, w_down_all)
    return h.reshape(bsz, seq, d)
```

```python
import functools
import math

import jax
import jax.numpy as jnp
from jax import lax
from jax.experimental import pallas as pl
from jax.experimental.pallas import tpu as pltpu

f32 = jnp.float32
bf16 = jnp.bfloat16

EPS = 1e-6
NEG_INF = -1e30
RG_C = 8.0
RG_HEADS = 8
RG_WIDTH = 512
S5_WIDTH = 256
S5_GROUP = 16
S5_STATE = 64
HG_HEADS = 4
HG_DK = 128
HG_CHUNK = 64
HG_W = HG_HEADS * HG_DK
DA_HEADS = 4
DA_HEAD_DIM = 64
DA_WIDTH = DA_HEADS * DA_HEAD_DIM
DA_PATTERNS = ((128, 1), (512, 4), (2048, 16))
DA_BLOCK = 128
DA_SPAN = 2048
LANES = 128
SUBLANES = 8
VMEM_LIMIT = 56 * 1024 * 1024


def _rms(x, g):
    return x * lax.rsqrt(jnp.mean(x * x, axis=-1, keepdims=True) + EPS) * g


def _dot(a, b):
    return jnp.dot(a, b, preferred_element_type=f32)


def _dot_nt(a, b):
    return lax.dot_general(a, b, (((1,), (1,)), ((), ())), preferred_element_type=f32)


def _dot_tn(a, b):
    return lax.dot_general(a, b, (((0,), (0,)), ((), ())), preferred_element_type=f32)


def _const_spec(shape):
    nd = len(shape)
    return pl.BlockSpec(shape, lambda *_: (0,) * nd, pipeline_mode=pl.Buffered(1))


def _params(sem):
    return pltpu.CompilerParams(dimension_semantics=sem, vmem_limit_bytes=VMEM_LIMIT)


PROJ_TM = 512
PROJ_CHUNK = 512


def _norm_proj_kernel(x_ref, g_ref, w_ref, *out_refs, layout):
    xn = _rms(x_ref[...], g_ref[...]).astype(bf16)
    col = 0
    for o_ref, (kind, width) in zip(out_refs, layout):
        for c0 in range(0, width, PROJ_CHUNK):
            cw = min(PROJ_CHUNK, width - c0)
            res = _dot(xn, w_ref[:, col + c0:col + c0 + cw])
            if kind == "flat":
                o_ref[:, c0:c0 + cw] = res
            else:
                for s in range(cw // LANES):
                    o_ref[c0 // LANES + s] = res[:, s * LANES:(s + 1) * LANES]
        col += width


def _norm_proj(h2d, g, w, layout):
    n, d = h2d.shape
    tm = PROJ_TM
    out_shape, out_specs = [], []
    for kind, width in layout:
        if kind == "flat":
            out_shape.append(jax.ShapeDtypeStruct((n, width), f32))
            out_specs.append(pl.BlockSpec((tm, width), lambda i: (i, 0)))
        else:
            out_shape.append(jax.ShapeDtypeStruct((width // LANES, n, LANES), f32))
            out_specs.append(pl.BlockSpec((width // LANES, tm, LANES), lambda i: (0, i, 0)))
    return pl.pallas_call(
        functools.partial(_norm_proj_kernel, layout=layout),
        grid=(n // tm,),
        in_specs=[pl.BlockSpec((tm, d), lambda i: (i, 0)),
                  _const_spec((1, d)),
                  _const_spec(w.shape)],
        out_specs=out_specs,
        out_shape=out_shape,
        compiler_params=_params(("arbitrary",)),
        name="norm_proj",
    )(h2d, g.reshape(1, d), w)


EV_T = 512
S5_L = 4
EV_PROJ_CHUNK = 256


def _even_kernel(h_ref, g_ref, win_ref, rcw_ref, rcb_ref, wa_ref, ba_ref, wx_ref, bx_ref, lam_ref,
                 ks_ref, bs_ref, ms_ref, lp_ref, pre_ref, pim_ref,
                 d5_ref, wg_ref, bg_ref,
                 ya_ref, yb_ref,
                 pj, xst, rga, rgb, rgc, ust, sre, sim, scr, sci, hst, yst):
    T = EV_T
    t = pl.program_id(1)

    @pl.when(t == 0)
    def _():
        xst[:, 0:SUBLANES, :] = jnp.zeros((RG_WIDTH // LANES, SUBLANES, LANES), f32)
        ust[:, 0:SUBLANES, :] = jnp.zeros((S5_WIDTH // LANES, SUBLANES, LANES), f32)
        rgc[...] = jnp.zeros_like(rgc)
        scr[...] = jnp.zeros_like(scr)
        sci[...] = jnp.zeros_like(sci)

    xn = _rms(h_ref[...], g_ref[...]).astype(bf16)
    for c0 in range(0, pj.shape[1], EV_PROJ_CHUNK):
        pj[:, c0:c0 + EV_PROJ_CHUNK] = _dot(xn, win_ref[:, c0:c0 + EV_PROJ_CHUNK])
    p_ref = pj

    kc = rcw_ref.shape[0]
    us = []
    for s in range(RG_WIDTH // LANES):
        cols = slice(s * LANES, (s + 1) * LANES)
        xst[s, SUBLANES:SUBLANES + T, :] = p_ref[:, cols]
        us_ = rcb_ref[:, cols]
        for k in range(kc):
            off = SUBLANES - (kc - 1) + k
            us_ = us_ + rcw_ref[k:k + 1, cols] * xst[s, off:off + T, :]
        us.append(us_)
    u = jnp.concatenate(us, axis=1)
    xst[:, 0:SUBLANES, :] = xst[:, T:T + SUBLANES, :]
    ub = u.astype(bf16)
    r = jax.nn.sigmoid(_dot(ub, wa_ref[...]) + ba_ref[...])
    ig = jax.nn.sigmoid(_dot(ub, wx_ref[...]) + bx_ref[...])
    log_a = (RG_C * r) * jax.nn.log_sigmoid(lam_ref[...])
    a = jnp.exp(log_a)
    bt = jnp.sqrt(-jnp.tanh(log_a) * (a * a + 1.0)) * (ig * u)
    ngrp = T // SUBLANES
    a = a.reshape(ngrp, SUBLANES, RG_WIDTH)
    bt = bt.reshape(ngrp, SUBLANES, RG_WIDTH)
    rowm = lax.broadcasted_iota(jnp.int32, (1, SUBLANES, RG_WIDTH), 1)
    for s in (1, 2, 4):
        m = rowm >= s
        bt = bt + jnp.where(m, a * pltpu.roll(bt, s, 1), 0.0)
        a = a * jnp.where(m, pltpu.roll(a, s, 1), 1.0)
    rga[...] = a.reshape(T, RG_WIDTH)
    rgb[...] = bt.reshape(T, RG_WIDTH)
    ya_ref[...] = jax.nn.gelu(p_ref[:, RG_WIDTH:2 * RG_WIDTH])

    L = S5_L
    nc = T // L
    ns = sre.shape[1]
    nsl = S5_WIDTH // LANES
    u5 = p_ref[:, 2 * RG_WIDTH:2 * RG_WIDTH + S5_WIDTH]
    for s in range(nsl):
        ust[s, SUBLANES:SUBLANES + T, :] = u5[:, s * LANES:(s + 1) * LANES]
    rowl = lax.broadcasted_iota(jnp.int32, (T, LANES), 0) & (L - 1)
    pieces, ends = [], []
    for k in range(L):
        for s in range(nsl):
            p = ust[s, SUBLANES - k:SUBLANES - k + T, :]
            if k:
                p = jnp.where(rowl >= k, p, 0.0)
            pieces.append(p.astype(bf16))
            ends.append(ust[s, pl.ds(SUBLANES + L - 1 - k, nc, stride=L), :].astype(bf16))
    yb_ref[...] = _dot(jnp.concatenate(pieces, axis=1), ks_ref[...]) + d5_ref[...] * u5
    gend = _dot(jnp.concatenate(ends, axis=1), bs_ref[...])
    ngrp5 = nc // SUBLANES
    hre = gend[:, 0:ns].reshape(ngrp5, SUBLANES, ns)
    him = gend[:, ns:2 * ns].reshape(ngrp5, SUBLANES, ns)
    for i, s in enumerate((1, 2, 4)):
        cr = lp_ref[2 * i][None]
        ci = lp_ref[2 * i + 1][None]
        zr = pltpu.roll(hre, s, 1)
        zi = pltpu.roll(him, s, 1)
        hre, him = hre + (cr * zr - ci * zi), him + (cr * zi + ci * zr)
    sre[...] = hre.reshape(nc, ns)
    sim[...] = him.reshape(nc, ns)
    hst[0:SUBLANES, 0:ns] = scr[...]
    hst[0:SUBLANES, ns:2 * ns] = sci[...]

    def rg_body(gidx, c):
        r0 = pl.multiple_of(gidx * SUBLANES, SUBLANES)
        h = rga[pl.ds(r0, SUBLANES), :] * c + rgb[pl.ds(r0, SUBLANES), :]
        rgb[pl.ds(r0, SUBLANES), :] = h
        return jnp.broadcast_to(h[SUBLANES - 1:SUBLANES, :], (SUBLANES, RG_WIDTH))

    rgc[...] = lax.fori_loop(0, T // SUBLANES, rg_body, rgc[...], unroll=4)

    def s5_body(gidx, c):
        cre, cim = c
        r0 = pl.multiple_of(gidx * SUBLANES, SUBLANES)
        pr = pre_ref[...]
        pi = pim_ref[...]
        hr = sre[pl.ds(r0, SUBLANES), :] + (pr * cre - pi * cim)
        hi = sim[pl.ds(r0, SUBLANES), :] + (pr * cim + pi * cre)
        sre[pl.ds(r0, SUBLANES), :] = hr
        sim[pl.ds(r0, SUBLANES), :] = hi
        return (jnp.broadcast_to(hr[SUBLANES - 1:SUBLANES, :], (SUBLANES, ns)),
                jnp.broadcast_to(hi[SUBLANES - 1:SUBLANES, :], (SUBLANES, ns)))

    cre, cim = lax.fori_loop(0, ngrp5, s5_body, (scr[...], sci[...]), unroll=2)
    scr[...] = cre
    sci[...] = cim
    hst[SUBLANES:SUBLANES + nc, 0:ns] = sre[...]
    hst[SUBLANES:SUBLANES + nc, ns:2 * ns] = sim[...]
    hprev = hst[SUBLANES - 1:SUBLANES - 1 + nc, :].astype(bf16)
    yi = _dot(hprev, ms_ref[...])
    for r in range(L):
        for s in range(nsl):
            c0 = r * S5_WIDTH + s * LANES
            yst[s, pl.ds(r, nc, stride=L), :] = yi[:, c0:c0 + LANES]
    ya_ref[...] = ya_ref[...] * rgb[...]
    y5 = yb_ref[...] + jnp.concatenate([yst[s] for s in range(nsl)], axis=1)
    v = jax.nn.gelu(y5)
    yb_ref[...] = v * jax.nn.sigmoid(_dot(v.astype(bf16), wg_ref[...]) + bg_ref[...])


def _block_diag(w):
    *lead, g, a, b = w.shape
    eye = jnp.eye(g, dtype=w.dtype)
    return (eye[:, None, :, None] * w[..., :, :, None, :]).reshape(*lead, g * a, g * b)


def _even_mixer(h2d, g0, w_in, bsz, seq, rg_conv_w, rg_conv_b, rg_w_a, rg_b_a, rg_w_x, rg_b_x, rg_lambda,
                s5_a_re, s5_a_im, s5_b_re, s5_b_im, s5_c_re, s5_c_im, s5_d, s5_log_dt,
                s5_w_glu, s5_b_glu):
    n, dm = h2d.shape
    T = EV_T
    L = S5_L
    nt = seq // T
    ns = s5_a_re.shape[0] * S5_STATE
    dt = jnp.exp(s5_log_dt.astype(f32))[:, None]
    ar = s5_a_re.astype(f32)
    ai = s5_a_im.astype(f32)
    mag = jnp.exp(ar * dt)
    abar_re = mag * jnp.cos(ai * dt)
    abar_im = mag * jnp.sin(ai * dt)
    den = ar * ar + ai * ai
    num_re = abar_re - 1.0
    f_re = (num_re * ar + abar_im * ai) / den
    f_im = (abar_im * ar - num_re * ai) / den
    br = s5_b_re.astype(f32)
    bi = s5_b_im.astype(f32)
    bb_re = f_re[..., None] * br - f_im[..., None] * bi
    bb_im = f_re[..., None] * bi + f_im[..., None] * br
    cr = s5_c_re.astype(f32)
    ci = s5_c_im.astype(f32)
    jj = jnp.arange(SUBLANES * L + 1, dtype=f32)[:, None, None]
    pmag = jnp.exp(jj * (ar * dt)[None])
    pang = jj * (ai * dt)[None]
    pwr = pmag * jnp.cos(pang)
    pwi = pmag * jnp.sin(pang)
    lr, li = pwr[:L, :, None, :], pwi[:L, :, None, :]
    clr = cr[None] * lr - ci[None] * li
    cli = cr[None] * li + ci[None] * lr
    kk = jnp.einsum('kgpn,gnq->kgqp', clr, bb_re) - jnp.einsum('kgpn,gnq->kgqp', cli, bb_im)
    ks = _block_diag(kk).reshape(L * S5_WIDTH, S5_WIDTH).astype(bf16)
    lbr = pwr[:L, :, :, None] * bb_re[None] - pwi[:L, :, :, None] * bb_im[None]
    lbi = pwr[:L, :, :, None] * bb_im[None] + pwi[:L, :, :, None] * bb_re[None]
    bs = jnp.concatenate([_block_diag(jnp.swapaxes(lbr, 2, 3)), _block_diag(jnp.swapaxes(lbi, 2, 3))],
                         axis=2).reshape(L * S5_WIDTH, 2 * ns).astype(bf16)
    l1r, l1i = pwr[1:L + 1, :, None, :], pwi[1:L + 1, :, None, :]
    c1r = cr[None] * l1r - ci[None] * l1i
    c1i = cr[None] * l1i + ci[None] * l1r
    ms = jnp.concatenate([_block_diag(jnp.swapaxes(c1r, 2, 3)), _block_diag(-jnp.swapaxes(c1i, 2, 3))], axis=1)
    ms = jnp.swapaxes(ms, 0, 1).reshape(2 * ns, L * S5_WIDTH).astype(bf16)
    shifts = jnp.asarray((1, 2, 4))
    rows = jnp.arange(SUBLANES)[None, :, None]
    keep = rows >= shifts[:, None, None]
    lp = jnp.stack([jnp.where(keep, p[L * shifts].reshape(3, 1, ns), 0.0) for p in (pwr, pwi)],
                   axis=1).reshape(6, SUBLANES, ns)
    p_re = pwr[L::L].reshape(SUBLANES, ns)
    p_im = pwi[L::L].reshape(SUBLANES, ns)
    wa_bd = _block_diag(rg_w_a).astype(bf16)
    wx_bd = _block_diag(rg_w_x).astype(bf16)
    row = lambda v: v.reshape(1, -1).astype(f32)
    consts = [row(g0), w_in.astype(bf16),
              rg_conv_w.astype(f32), row(rg_conv_b), wa_bd, row(rg_b_a), wx_bd, row(rg_b_x), row(rg_lambda),
              ks, bs, ms, lp, p_re, p_im, row(s5_d),
              s5_w_glu.astype(bf16), row(s5_b_glu)]
    width = w_in.shape[1]
    nc = T // L
    return pl.pallas_call(
        _even_kernel,
        grid=(bsz, nt),
        in_specs=[pl.BlockSpec((T, dm), lambda b, t: (b * nt + t, 0))]
                 + [_const_spec(c.shape) for c in consts],
        out_specs=[pl.BlockSpec((T, RG_WIDTH), lambda b, t: (b * nt + t, 0)),
                   pl.BlockSpec((T, S5_WIDTH), lambda b, t: (b * nt + t, 0))],
        out_shape=[jax.ShapeDtypeStruct((n, RG_WIDTH), f32),
                   jax.ShapeDtypeStruct((n, S5_WIDTH), f32)],
        scratch_shapes=[pltpu.VMEM((T, width), f32),
                        pltpu.VMEM((RG_WIDTH // LANES, T + SUBLANES, LANES), f32),
                        pltpu.VMEM((T, RG_WIDTH), f32), pltpu.VMEM((T, RG_WIDTH), f32),
                        pltpu.VMEM((SUBLANES, RG_WIDTH), f32),
                        pltpu.VMEM((S5_WIDTH // LANES, T + SUBLANES, LANES), f32),
                        pltpu.VMEM((nc, ns), f32), pltpu.VMEM((nc, ns), f32),
                        pltpu.VMEM((SUBLANES, ns), f32), pltpu.VMEM((SUBLANES, ns), f32),
                        pltpu.VMEM((SUBLANES + nc, 2 * ns), f32),
                        pltpu.VMEM((S5_WIDTH // LANES, T, LANES), f32)],
        compiler_params=_params(("arbitrary", "arbitrary")),
        name="even_mixer",
    )(h2d, *consts)


HG_T = 512
HG_GROUP = 4
HG_UNROLL = 2


def _split3(x):
    hi = x.astype(bf16)
    r1 = x - hi.astype(f32)
    mid = r1.astype(bf16)
    lo = (r1 - mid.astype(f32)).astype(bf16)
    return hi, mid, lo


def _hgrn_kernel(p_ref, lb_ref, nw_ref, o_ref, st_ref):
    C = HG_CHUNK
    t = pl.program_id(1)

    @pl.when(t == 0)
    def _():
        st_ref[...] = jnp.zeros_like(st_ref)

    ri = lax.broadcasted_iota(jnp.int32, (C, C), 0)
    ci = lax.broadcasted_iota(jnp.int32, (C, C), 1)
    causal = ri >= ci
    tril = causal.astype(bf16)
    lb = lb_ref[...]

    heads = range(HG_HEADS)
    sls = [slice(h * HG_DK, (h + 1) * HG_DK) for h in heads]

    def group(gidx, carry):
        prep = []
        for ci_ in range(HG_GROUP):
            r0 = pl.multiple_of((gidx * HG_GROUP + ci_) * C, C)
            q = p_ref[pl.ds(r0, C), 0:HG_W]
            z = p_ref[pl.ds(r0, C), HG_W:2 * HG_W]
            sg = jax.nn.sigmoid(z)
            log_f = jnp.log(lb + (1.0 - lb) * sg)
            kk = (1.0 - lb) * (1.0 - sg)
            hi, mid_, lo = _split3(log_f)
            cum = _dot(tril, hi) + _dot(tril, mid_) + _dot(tril, lo)
            prep.append((r0, jax.nn.silu(q), kk, cum))
        mats = []
        for r0, qs, kk, cum in prep:
            midv = cum[C // 2:C // 2 + 1, :]
            last = cum[C - 1:C, :]
            qa = (qs * jnp.exp(cum - midv)).astype(bf16)
            kb = (kk * jnp.exp(midv - cum)).astype(bf16)
            qi = (qs * jnp.exp(cum)).astype(bf16)
            kc = (kk * jnp.exp(last - cum)).astype(bf16)
            v = [p_ref[pl.ds(r0, C), 2 * HG_W + h * HG_DK:2 * HG_W + (h + 1) * HG_DK].astype(bf16)
                 for h in heads]
            sc = [_dot_nt(qa[:, sl], kb[:, sl]) for sl in sls]
            con_t = [_dot_tn(v[h], kc[:, sl]) for h, sl in enumerate(sls)]
            mats.append((r0, qi, v, sc, con_t, jnp.exp(last)))
        st = [st_ref[h] for h in heads]
        outs = []
        for r0, qi, v, sc, con_t, dec in mats:
            oi = [_dot_nt(qi[:, sl], st[h].astype(bf16)) for h, sl in enumerate(sls)]
            st = [dec[:, sl] * st[h] + con_t[h] for h, sl in enumerate(sls)]
            outs.append((r0, v, sc, oi))
        for h in heads:
            st_ref[h] = st[h]
        for r0, v, sc, oi in outs:
            o = [_dot(jnp.where(causal, sc[h], 0.0).astype(bf16), v[h]) + oi[h] for h in heads]
            for h, sl in enumerate(sls):
                g_h = p_ref[pl.ds(r0, C), 3 * HG_W + h * HG_DK:3 * HG_W + (h + 1) * HG_DK]
                on = o[h] * lax.rsqrt(jnp.mean(o[h] * o[h], axis=-1, keepdims=True) + EPS) * nw_ref[...]
                o_ref[pl.ds(r0, C), sl] = on * jax.nn.silu(g_h)
        return carry

    lax.fori_loop(0, HG_T // (C * HG_GROUP), group, 0, unroll=HG_UNROLL)


def _hgrn_mixer(hg, bsz, seq, lb, norm_w):
    n = hg.shape[0]
    T = HG_T
    nt = seq // T
    return pl.pallas_call(
        _hgrn_kernel,
        grid=(bsz, nt),
        in_specs=[pl.BlockSpec((T, 4 * HG_W), lambda b, t: (b * nt + t, 0)),
                  _const_spec((1, HG_W)), _const_spec((1, HG_DK))],
        out_specs=pl.BlockSpec((T, HG_W), lambda b, t: (b * nt + t, 0)),
        out_shape=jax.ShapeDtypeStruct((n, HG_W), f32),
        scratch_shapes=[pltpu.VMEM((HG_HEADS, HG_DK, HG_DK), f32)],
        compiler_params=_params(("arbitrary", "arbitrary")),
        name="hgrn_mixer",
    )(hg, lb.reshape(1, HG_W).astype(f32), norm_w.reshape(1, HG_DK).astype(f32))


N_PAIR = DA_WIDTH // LANES
DA_MERGE_ROWS = 256


def _attn_group(gi, d, n, sd_ref, x_ref, qd, kv, og, lg, bias_scr):
    B = DA_BLOCK
    L = DA_SPAN // d
    nblk = L // B
    nblk_log = int(math.log2(nblk))
    stride_r = L + B

    @pl.when(n == 0)
    def _():
        for r in range(d):
            kv[:, r * stride_r:r * stride_r + B, :] = jnp.zeros((2 * N_PAIR, B, LANES), bf16)

    scale = DA_HEAD_DIM ** -0.5
    for r in range(d):
        rows = pl.ds(r, L, stride=d) if d > 1 else pl.ds(0, L)
        for s in range(N_PAIR):
            qd[s, r * L:(r + 1) * L, :] = (x_ref[s, rows, :] * scale).astype(bf16)
        for s in range(2 * N_PAIR):
            kv[s, r * stride_r + B:r * stride_r + B + L, :] = x_ref[N_PAIR + s, rows, :].astype(bf16)

    qi = lax.broadcasted_iota(jnp.int32, (B, 2 * B), 0)
    kj = lax.broadcasted_iota(jnp.int32, (B, 2 * B), 1)
    rel = qi + B - kj
    valid = (rel >= 0) & (rel <= B)
    relf = rel.astype(f32)
    for h in range(DA_HEADS):
        bias = jnp.where(valid, -sd_ref[gi, h] * relf, NEG_INF)
        bias_scr[0, h] = bias
        bias_scr[1, h] = jnp.where(kj < B, NEG_INF, bias)

    lane = lax.broadcasted_iota(jnp.int32, (B, LANES), 1)
    lo_half = lane < DA_HEAD_DIM

    def blk_body(blk, carry):
        r = blk >> nblk_log
        m = blk & (nblk - 1)
        q0 = pl.multiple_of(blk * B, B)
        k0 = pl.multiple_of((blk + r) * B, B)
        first = jnp.logical_and(n == 0, m == 0).astype(jnp.int32)
        start = m * (B * d) + r
        orow = pl.ds(start, B, stride=d) if d > 1 else pl.ds(pl.multiple_of(start, B), B)
        for j in range(N_PAIR):
            qp = qd[j, pl.ds(q0, B), :]
            kp = kv[j, pl.ds(k0, 2 * B), :]
            vp = kv[N_PAIR + j, pl.ds(k0, 2 * B), :]
            outs, lses = [], []
            for e in range(2):
                keep = lo_half if e == 0 else jnp.logical_not(lo_half)
                qm = jnp.where(keep, qp, jnp.zeros_like(qp))
                s = _dot_nt(qm, kp) + bias_scr[first, 2 * j + e]
                mx = jnp.max(s, axis=-1, keepdims=True)
                ex = jnp.exp(s - mx)
                l = jnp.sum(ex, axis=-1, keepdims=True)
                outs.append(_dot(ex.astype(bf16), vp) * (1.0 / l))
                lses.append(jnp.broadcast_to(mx + jnp.log(l), (B, LANES)))
            og[gi, j, orow, :] = jnp.where(lo_half, outs[0], outs[1])
            lg[gi, j, orow, :] = jnp.where(lo_half, lses[0], lses[1])
        return carry

    lax.fori_loop(0, DA_SPAN // B, blk_body, 0, unroll=8)

    for r in range(d):
        kv[:, r * stride_r:r * stride_r + B, :] = kv[:, r * stride_r + L:r * stride_r + L + B, :]


def _attn_kernel(sd_ref, x_ref, o_ref, qd, kv0, kv1, kv2, og, lg, bias_scr):
    n = pl.program_id(1)
    g = pl.program_id(2)
    kvs = (kv0, kv1, kv2)
    for gi, (_, d) in enumerate(DA_PATTERNS):
        @pl.when(g == gi)
        def _(gi=gi, d=d):
            _attn_group(gi, d, n, sd_ref, x_ref, qd, kvs[gi], og, lg, bias_scr)

    @pl.when(g == len(DA_PATTERNS) - 1)
    def _():
        R = DA_MERGE_ROWS

        def merge(i, carry):
            r0 = pl.multiple_of(i * R, R)
            for j in range(N_PAIR):
                ls = [lg[gi, j, pl.ds(r0, R), :] for gi in range(len(DA_PATTERNS))]
                mx = functools.reduce(jnp.maximum, ls)
                ws = [jnp.exp(l - mx) for l in ls]
                den = functools.reduce(lambda a, b: a + b, ws)
                num = functools.reduce(lambda a, b: a + b,
                                       [w * og[gi, j, pl.ds(r0, R), :] for gi, w in enumerate(ws)])
                o_ref[pl.ds(r0, R), j * LANES:(j + 1) * LANES] = num * (1.0 / den)
            return carry

        lax.fori_loop(0, DA_SPAN // R, merge, 0)


def _attn_mixer(qkv_slabs, bsz, seq, sd):
    ng = len(DA_PATTERNS)
    n = qkv_slabs.shape[1]
    nt = seq // DA_SPAN
    per_g = 3 * N_PAIR
    kv_scratch = [pltpu.VMEM((2 * N_PAIR, DA_SPAN + DA_BLOCK * d, LANES), bf16) for _, d in DA_PATTERNS]
    return pl.pallas_call(
        _attn_kernel,
        grid=(bsz, nt, ng),
        in_specs=[pl.BlockSpec(memory_space=pltpu.SMEM),
                  pl.BlockSpec((per_g, DA_SPAN, LANES), lambda b, t, g: (g, b * nt + t, 0))],
        out_specs=pl.BlockSpec((DA_SPAN, DA_WIDTH), lambda b, t, g: (b * nt + t, 0)),
        out_shape=jax.ShapeDtypeStruct((n, DA_WIDTH), f32),
        scratch_shapes=[pltpu.VMEM((N_PAIR, DA_SPAN, LANES), bf16)] + kv_scratch + [
            pltpu.VMEM((ng, N_PAIR, DA_SPAN, LANES), f32),
            pltpu.VMEM((ng, N_PAIR, DA_SPAN, LANES), f32),
            pltpu.VMEM((2, DA_HEADS, DA_BLOCK, 2 * DA_BLOCK), f32)],
        compiler_params=_params(("arbitrary", "arbitrary", "arbitrary")),
        name="dilated_attn",
    )(sd, qkv_slabs)


POST_SUB = 256
POST_NSUB = 2
POST_TM = POST_SUB * POST_NSUB
FF_CHUNK = 256


def _post_kernel(ya_ref, yb_ref, h_ref, woa_ref, wob_ref, g_ref, wup_ref, cw_ref, cb_ref, wdn_ref,
                 o_ref, xn_scr, st_scr, carry_scr, acc_scr, act_scr):
    tm = POST_SUB
    C = FF_CHUNK
    dff = wdn_ref.shape[0]
    kc = cw_ref.shape[0]
    t = pl.program_id(1)
    nchunk = dff // C
    ns = C // LANES

    @pl.when(t == 0)
    def _():
        carry_scr[...] = jnp.zeros_like(carry_scr)

    def prologue(i):
        rows = pl.ds(i * tm, tm)
        y1 = (_dot(ya_ref[rows, :].astype(bf16), woa_ref[...])
              + _dot(yb_ref[rows, :].astype(bf16), wob_ref[...]))
        h1 = h_ref[rows, :] + _rms(y1, g_ref[1:2, :])
        o_ref[rows, :] = h1
        xn_scr[i] = _rms(h1, g_ref[2:3, :]).astype(bf16)
        acc_scr[i] = jnp.zeros(acc_scr.shape[1:], f32)

    def up(i, j):
        slot = j % 2
        xn = xn_scr[i]
        c0 = j * C
        rv = _dot(xn, wup_ref[:, c0:c0 + C])
        rg = _dot(xn, wup_ref[:, dff + c0:dff + c0 + C])
        st_scr[slot, :, 0:SUBLANES, :] = carry_scr[j]
        for s in range(ns):
            st_scr[slot, s, SUBLANES:SUBLANES + tm, :] = rv[:, s * LANES:(s + 1) * LANES]
            st_scr[slot, ns + s, SUBLANES:SUBLANES + tm, :] = rg[:, s * LANES:(s + 1) * LANES]

    def act(j):
        slot = j % 2
        parts = []
        for s in range(ns):
            c0 = j * C + s * LANES
            cv = cb_ref[:, c0:c0 + LANES]
            cg = cb_ref[:, dff + c0:dff + c0 + LANES]
            for k in range(kc):
                off = SUBLANES - (kc - 1) + k
                cv = cv + cw_ref[k:k + 1, c0:c0 + LANES] * st_scr[slot, s, off:off + tm, :]
                cg = cg + cw_ref[k:k + 1, dff + c0:dff + c0 + LANES] * st_scr[slot, ns + s, off:off + tm, :]
            parts.append((jax.nn.gelu(cg) * cv).astype(bf16))
        act_scr[slot] = jnp.concatenate(parts, axis=1)
        carry_scr[j] = st_scr[slot, :, tm:tm + SUBLANES, :]

    def down(i, j):
        acc_scr[i] += _dot(act_scr[j % 2], wdn_ref[j * C:(j + 1) * C, :])

    def ffn(i):
        up(i, 0)
        for j in range(nchunk):
            if j + 1 < nchunk:
                up(i, j + 1)
            if j >= 1:
                down(i, j - 1)
            act(j)
        down(i, nchunk - 1)

    def epilogue(i):
        rows = pl.ds(i * tm, tm)
        o_ref[rows, :] = o_ref[rows, :] + _rms(acc_scr[i], g_ref[3:4, :])

    for i in range(POST_NSUB):
        prologue(i)
    for i in range(POST_NSUB):
        ffn(i)
        epilogue(i)


def _layer_spec(shape, layer):
    nd = len(shape) - 1
    return pl.BlockSpec((None,) + tuple(shape[1:]), lambda *_: (layer,) + (0,) * nd,
                        pipeline_mode=pl.Buffered(1))


def _post(ya, yb, h2d, bsz, seq, w_out, layer, g_all, w_up_all, conv_w_all, conv_b_all, w_down_all):
    n, d = h2d.shape
    tm = POST_TM
    nt = seq // tm
    wa = ya.shape[1]
    wb = yb.shape[1]
    dff = w_down_all.shape[1]
    woa = w_out[:wa].astype(bf16)
    wob = w_out[wa:].astype(bf16)
    stacked = [g_all, w_up_all, conv_w_all, conv_b_all, w_down_all]
    consts = [woa, wob] + stacked
    row_spec = lambda w: pl.BlockSpec((tm, w), lambda b, t: (b * nt + t, 0))
    return pl.pallas_call(
        _post_kernel,
        grid=(bsz, nt),
        in_specs=[row_spec(wa), row_spec(wb), row_spec(d), _const_spec(woa.shape), _const_spec(wob.shape)]
                 + [_layer_spec(c.shape, layer) for c in stacked],
        out_specs=row_spec(d),
        out_shape=jax.ShapeDtypeStruct((n, d), f32),
        scratch_shapes=[pltpu.VMEM((POST_NSUB, POST_SUB, d), bf16),
                        pltpu.VMEM((2, 2 * FF_CHUNK // LANES, POST_SUB + SUBLANES, LANES), f32),
                        pltpu.VMEM((dff // FF_CHUNK, 2 * FF_CHUNK // LANES, SUBLANES, LANES), f32),
                        pltpu.VMEM((POST_NSUB, POST_SUB, d), f32),
                        pltpu.VMEM((2, POST_SUB, FF_CHUNK), bf16)],
        compiler_params=_params(("arbitrary", "arbitrary")),
        name="post_ffn",
    )(ya, yb, h2d, *consts)


def kernel(x, norm_g, ffn_w_up, ffn_conv_w, ffn_conv_b, ffn_w_down, ev_w_in, ev_w_out, rg_conv_w, rg_conv_b,
           rg_w_a, rg_b_a, rg_w_x, rg_b_x, rg_lambda, s5_a_re, s5_a_im, s5_b_re, s5_b_im, s5_c_re, s5_c_im,
           s5_d, s5_log_dt, s5_w_glu, s5_b_glu, od_w_in, od_w_out, hg_lower, hg_norm_g):
    bsz, seq, d = x.shape
    depth = norm_g.shape[0]
    assert seq % DA_SPAN == 0 and d % LANES == 0
    lb_p = jax.nn.softmax(hg_lower.astype(f32), axis=0)
    lb_all = jnp.cumsum(lb_p, axis=0) - lb_p[0]
    ngrp = len(DA_PATTERNS)
    slopes = (2.0 ** (-8.0 * jnp.arange(1, ngrp * DA_HEADS + 1, dtype=f32) / (ngrp * DA_HEADS))
              ).reshape(ngrp, DA_HEADS)
    sd = slopes * jnp.asarray([float(p[1]) for p in DA_PATTERNS], f32)[:, None]

    g_all = norm_g.astype(f32)
    w_up_all = ffn_w_up.astype(bf16)
    w_down_all = ffn_w_down.astype(bf16)
    conv_w_all = ffn_conv_w.astype(f32)
    conv_b_all = ffn_conv_b.astype(f32).reshape(depth, 1, -1)

    h = x.reshape(bsz * seq, d)
    for layer in range(depth):
        j = layer // 2
        g = norm_g[layer]
        if layer % 2 == 0:
            ya, yb = _even_mixer(h, g[0], ev_w_in[j], bsz, seq, rg_conv_w[j], rg_conv_b[j], rg_w_a[j], rg_b_a[j],
                                 rg_w_x[j], rg_b_x[j], rg_lambda[j], s5_a_re[j], s5_a_im[j],
                                 s5_b_re[j], s5_b_im[j], s5_c_re[j], s5_c_im[j], s5_d[j], s5_log_dt[j],
                                 s5_w_glu[j], s5_b_glu[j])
            w_out = ev_w_out[j]
        else:
            hgp, qkv = _norm_proj(h, g[0], od_w_in[j].astype(bf16),
                                  (("flat", 4 * HG_W), ("slab", ngrp * 3 * DA_WIDTH)))
            ya = _hgrn_mixer(hgp, bsz, seq, lb_all[layer], hg_norm_g[j])
            yb = _attn_mixer(qkv, bsz, seq, sd)
            w_out = od_w_out[j]
        h = _post(ya, yb, h, bsz, seq, w_out, layer, g_all, w_up_all, conv_w_all, conv_b_all, w_down_all)
    return h.reshape(bsz, seq, d)
```

```python
import functools
import math

import jax
import jax.numpy as jnp
from jax import lax
from jax.experimental import pallas as pl
from jax.experimental.pallas import tpu as pltpu

f32 = jnp.float32
bf16 = jnp.bfloat16

EPS = 1e-6
NEG_INF = -1e30
RG_C = 8.0
RG_HEADS = 8
RG_WIDTH = 512
S5_WIDTH = 256
S5_GROUP = 16
S5_STATE = 64
HG_HEADS = 4
HG_DK = 128
HG_CHUNK = 64
HG_W = HG_HEADS * HG_DK
DA_HEADS = 4
DA_HEAD_DIM = 64
DA_WIDTH = DA_HEADS * DA_HEAD_DIM
DA_PATTERNS = ((128, 1), (512, 4), (2048, 16))
DA_BLOCK = 128
DA_SPAN = 2048
LANES = 128
SUBLANES = 8
VMEM_LIMIT = 56 * 1024 * 1024


def _rms(x, g):
    return x * lax.rsqrt(jnp.mean(x * x, axis=-1, keepdims=True) + EPS) * g


def _dot(a, b):
    return jnp.dot(a, b, preferred_element_type=f32)


def _dot_nt(a, b):
    return lax.dot_general(a, b, (((1,), (1,)), ((), ())), preferred_element_type=f32)


def _dot_tn(a, b):
    return lax.dot_general(a, b, (((0,), (0,)), ((), ())), preferred_element_type=f32)


def _const_spec(shape):
    nd = len(shape)
    return pl.BlockSpec(shape, lambda *_: (0,) * nd, pipeline_mode=pl.Buffered(1))


def _params(sem):
    return pltpu.CompilerParams(dimension_semantics=sem, vmem_limit_bytes=VMEM_LIMIT)


PROJ_TM = 512
PROJ_CHUNK = 512


def _norm_proj_kernel(x_ref, g_ref, w_ref, *out_refs, layout):
    xn = _rms(x_ref[...], g_ref[...]).astype(bf16)
    col = 0
    for o_ref, (kind, width) in zip(out_refs, layout):
        for c0 in range(0, width, PROJ_CHUNK):
            cw = min(PROJ_CHUNK, width - c0)
            res = _dot(xn, w_ref[:, col + c0:col + c0 + cw])
            if kind == "flat":
                o_ref[:, c0:c0 + cw] = res
            else:
                for s in range(cw // LANES):
                    o_ref[c0 // LANES + s] = res[:, s * LANES:(s + 1) * LANES]
        col += width


def _norm_proj(h2d, g, w, layout):
    n, d = h2d.shape
    tm = PROJ_TM
    out_shape, out_specs = [], []
    for kind, width in layout:
        if kind == "flat":
            out_shape.append(jax.ShapeDtypeStruct((n, width), f32))
            out_specs.append(pl.BlockSpec((tm, width), lambda i: (i, 0)))
        else:
            out_shape.append(jax.ShapeDtypeStruct((width // LANES, n, LANES), f32))
            out_specs.append(pl.BlockSpec((width // LANES, tm, LANES), lambda i: (0, i, 0)))
    return pl.pallas_call(
        functools.partial(_norm_proj_kernel, layout=layout),
        grid=(n // tm,),
        in_specs=[pl.BlockSpec((tm, d), lambda i: (i, 0)),
                  _const_spec((1, d)),
                  _const_spec(w.shape)],
        out_specs=out_specs,
        out_shape=out_shape,
        compiler_params=_params(("arbitrary",)),
        name="norm_proj",
    )(h2d, g.reshape(1, d), w)


EV_T = 512
S5_L = 4
EV_PROJ_CHUNK = 256


def _even_kernel(h_ref, g_ref, win_ref, rcw_ref, rcb_ref, wa_ref, ba_ref, wx_ref, bx_ref, lam_ref,
                 ks_ref, bs_ref, ms_ref, lp_ref, pre_ref, pim_ref,
                 d5_ref, wg_ref, bg_ref,
                 ya_ref, yb_ref,
                 pj, xst, rga, rgb, rgc, ust, sre, sim, scr, sci, hst, yst):
    T = EV_T
    t = pl.program_id(1)

    @pl.when(t == 0)
    def _():
        xst[:, 0:SUBLANES, :] = jnp.zeros((RG_WIDTH // LANES, SUBLANES, LANES), f32)
        ust[:, 0:SUBLANES, :] = jnp.zeros((S5_WIDTH // LANES, SUBLANES, LANES), f32)
        rgc[...] = jnp.zeros_like(rgc)
        scr[...] = jnp.zeros_like(scr)
        sci[...] = jnp.zeros_like(sci)

    xn = _rms(h_ref[...], g_ref[...]).astype(bf16)
    for c0 in range(0, pj.shape[1], EV_PROJ_CHUNK):
        pj[:, c0:c0 + EV_PROJ_CHUNK] = _dot(xn, win_ref[:, c0:c0 + EV_PROJ_CHUNK])
    p_ref = pj

    kc = rcw_ref.shape[0]
    us = []
    for s in range(RG_WIDTH // LANES):
        cols = slice(s * LANES, (s + 1) * LANES)
        xst[s, SUBLANES:SUBLANES + T, :] = p_ref[:, cols]
        us_ = rcb_ref[:, cols]
        for k in range(kc):
            off = SUBLANES - (kc - 1) + k
            us_ = us_ + rcw_ref[k:k + 1, cols] * xst[s, off:off + T, :]
        us.append(us_)
    u = jnp.concatenate(us, axis=1)
    xst[:, 0:SUBLANES, :] = xst[:, T:T + SUBLANES, :]
    ub = u.astype(bf16)
    r = jax.nn.sigmoid(_dot(ub, wa_ref[...]) + ba_ref[...])
    ig = jax.nn.sigmoid(_dot(ub, wx_ref[...]) + bx_ref[...])
    log_a = (RG_C * r) * jax.nn.log_sigmoid(lam_ref[...])
    a = jnp.exp(log_a)
    bt = jnp.sqrt(-jnp.tanh(log_a) * (a * a + 1.0)) * (ig * u)
    ngrp = T // SUBLANES
    a = a.reshape(ngrp, SUBLANES, RG_WIDTH)
    bt = bt.reshape(ngrp, SUBLANES, RG_WIDTH)
    rowm = lax.broadcasted_iota(jnp.int32, (1, SUBLANES, RG_WIDTH), 1)
    for s in (1, 2, 4):
        m = rowm >= s
        bt = bt + jnp.where(m, a * pltpu.roll(bt, s, 1), 0.0)
        a = a * jnp.where(m, pltpu.roll(a, s, 1), 1.0)
    rga[...] = a.reshape(T, RG_WIDTH)
    rgb[...] = bt.reshape(T, RG_WIDTH)
    ya_ref[...] = jax.nn.gelu(p_ref[:, RG_WIDTH:2 * RG_WIDTH])

    L = S5_L
    nc = T // L
    ns = sre.shape[1]
    nsl = S5_WIDTH // LANES
    u5 = p_ref[:, 2 * RG_WIDTH:2 * RG_WIDTH + S5_WIDTH]
    for s in range(nsl):
        ust[s, SUBLANES:SUBLANES + T, :] = u5[:, s * LANES:(s + 1) * LANES]
    rowl = lax.broadcasted_iota(jnp.int32, (T, LANES), 0) & (L - 1)
    pieces, ends = [], []
    for k in range(L):
        for s in range(nsl):
            p = ust[s, SUBLANES - k:SUBLANES - k + T, :]
            if k:
                p = jnp.where(rowl >= k, p, 0.0)
            pieces.append(p.astype(bf16))
            ends.append(ust[s, pl.ds(SUBLANES + L - 1 - k, nc, stride=L), :].astype(bf16))
    yb_ref[...] = _dot(jnp.concatenate(pieces, axis=1), ks_ref[...]) + d5_ref[...] * u5
    gend = _dot(jnp.concatenate(ends, axis=1), bs_ref[...])
    ngrp5 = nc // SUBLANES
    hre = gend[:, 0:ns].reshape(ngrp5, SUBLANES, ns)
    him = gend[:, ns:2 * ns].reshape(ngrp5, SUBLANES, ns)
    for i, s in enumerate((1, 2, 4)):
        cr = lp_ref[2 * i][None]
        ci = lp_ref[2 * i + 1][None]
        zr = pltpu.roll(hre, s, 1)
        zi = pltpu.roll(him, s, 1)
        hre, him = hre + (cr * zr - ci * zi), him + (cr * zi + ci * zr)
    sre[...] = hre.reshape(nc, ns)
    sim[...] = him.reshape(nc, ns)
    hst[0:SUBLANES, 0:ns] = scr[...]
    hst[0:SUBLANES, ns:2 * ns] = sci[...]

    def rg_body(gidx, c):
        r0 = gidx * SUBLANES
        h = rga[pl.ds(r0, SUBLANES), :] * c + rgb[pl.ds(r0, SUBLANES), :]
        rgb[pl.ds(r0, SUBLANES), :] = h
        return jnp.broadcast_to(h[SUBLANES - 1:SUBLANES, :], (SUBLANES, RG_WIDTH))

    c = rgc[...]
    for gidx in range(T // SUBLANES):
        c = rg_body(gidx, c)
    rgc[...] = c

    def s5_body(gidx, c):
        cre, cim = c
        r0 = gidx * SUBLANES
        pr = pre_ref[...]
        pi = pim_ref[...]
        hr = sre[pl.ds(r0, SUBLANES), :] + (pr * cre - pi * cim)
        hi = sim[pl.ds(r0, SUBLANES), :] + (pr * cim + pi * cre)
        sre[pl.ds(r0, SUBLANES), :] = hr
        sim[pl.ds(r0, SUBLANES), :] = hi
        return (jnp.broadcast_to(hr[SUBLANES - 1:SUBLANES, :], (SUBLANES, ns)),
                jnp.broadcast_to(hi[SUBLANES - 1:SUBLANES, :], (SUBLANES, ns)))

    cc = (scr[...], sci[...])
    for gidx in range(ngrp5):
        cc = s5_body(gidx, cc)
    cre, cim = cc
    scr[...] = cre
    sci[...] = cim
    hst[SUBLANES:SUBLANES + nc, 0:ns] = sre[...]
    hst[SUBLANES:SUBLANES + nc, ns:2 * ns] = sim[...]
    hprev = hst[SUBLANES - 1:SUBLANES - 1 + nc, :].astype(bf16)
    yi = _dot(hprev, ms_ref[...])
    for r in range(L):
        for s in range(nsl):
            c0 = r * S5_WIDTH + s * LANES
            yst[s, pl.ds(r, nc, stride=L), :] = yi[:, c0:c0 + LANES]
    ya_ref[...] = ya_ref[...] * rgb[...]
    y5 = yb_ref[...] + jnp.concatenate([yst[s] for s in range(nsl)], axis=1)
    v = jax.nn.gelu(y5)
    yb_ref[...] = v * jax.nn.sigmoid(_dot(v.astype(bf16), wg_ref[...]) + bg_ref[...])


def _block_diag(w):
    *lead, g, a, b = w.shape
    eye = jnp.eye(g, dtype=w.dtype)
    return (eye[:, None, :, None] * w[..., :, :, None, :]).reshape(*lead, g * a, g * b)


def _even_mixer(h2d, g0, w_in, bsz, seq, rg_conv_w, rg_conv_b, rg_w_a, rg_b_a, rg_w_x, rg_b_x, rg_lambda,
                s5_a_re, s5_a_im, s5_b_re, s5_b_im, s5_c_re, s5_c_im, s5_d, s5_log_dt,
                s5_w_glu, s5_b_glu):
    n, dm = h2d.shape
    T = EV_T
    L = S5_L
    nt = seq // T
    ns = s5_a_re.shape[0] * S5_STATE
    dt = jnp.exp(s5_log_dt.astype(f32))[:, None]
    ar = s5_a_re.astype(f32)
    ai = s5_a_im.astype(f32)
    mag = jnp.exp(ar * dt)
    abar_re = mag * jnp.cos(ai * dt)
    abar_im = mag * jnp.sin(ai * dt)
    den = ar * ar + ai * ai
    num_re = abar_re - 1.0
    f_re = (num_re * ar + abar_im * ai) / den
    f_im = (abar_im * ar - num_re * ai) / den
    br = s5_b_re.astype(f32)
    bi = s5_b_im.astype(f32)
    bb_re = f_re[..., None] * br - f_im[..., None] * bi
    bb_im = f_re[..., None] * bi + f_im[..., None] * br
    cr = s5_c_re.astype(f32)
    ci = s5_c_im.astype(f32)
    jj = jnp.arange(SUBLANES * L + 1, dtype=f32)[:, None, None]
    pmag = jnp.exp(jj * (ar * dt)[None])
    pang = jj * (ai * dt)[None]
    pwr = pmag * jnp.cos(pang)
    pwi = pmag * jnp.sin(pang)
    lr, li = pwr[:L, :, None, :], pwi[:L, :, None, :]
    clr = cr[None] * lr - ci[None] * li
    cli = cr[None] * li + ci[None] * lr
    kk = jnp.einsum('kgpn,gnq->kgqp', clr, bb_re) - jnp.einsum('kgpn,gnq->kgqp', cli, bb_im)
    ks = _block_diag(kk).reshape(L * S5_WIDTH, S5_WIDTH).astype(bf16)
    lbr = pwr[:L, :, :, None] * bb_re[None] - pwi[:L, :, :, None] * bb_im[None]
    lbi = pwr[:L, :, :, None] * bb_im[None] + pwi[:L, :, :, None] * bb_re[None]
    bs = jnp.concatenate([_block_diag(jnp.swapaxes(lbr, 2, 3)), _block_diag(jnp.swapaxes(lbi, 2, 3))],
                         axis=2).reshape(L * S5_WIDTH, 2 * ns).astype(bf16)
    l1r, l1i = pwr[1:L + 1, :, None, :], pwi[1:L + 1, :, None, :]
    c1r = cr[None] * l1r - ci[None] * l1i
    c1i = cr[None] * l1i + ci[None] * l1r
    ms = jnp.concatenate([_block_diag(jnp.swapaxes(c1r, 2, 3)), _block_diag(-jnp.swapaxes(c1i, 2, 3))], axis=1)
    ms = jnp.swapaxes(ms, 0, 1).reshape(2 * ns, L * S5_WIDTH).astype(bf16)
    shifts = jnp.asarray((1, 2, 4))
    rows = jnp.arange(SUBLANES)[None, :, None]
    keep = rows >= shifts[:, None, None]
    lp = jnp.stack([jnp.where(keep, p[L * shifts].reshape(3, 1, ns), 0.0) for p in (pwr, pwi)],
                   axis=1).reshape(6, SUBLANES, ns)
    p_re = pwr[L::L].reshape(SUBLANES, ns)
    p_im = pwi[L::L].reshape(SUBLANES, ns)
    wa_bd = _block_diag(rg_w_a).astype(bf16)
    wx_bd = _block_diag(rg_w_x).astype(bf16)
    row = lambda v: v.reshape(1, -1).astype(f32)
    consts = [row(g0), w_in.astype(bf16),
              rg_conv_w.astype(f32), row(rg_conv_b), wa_bd, row(rg_b_a), wx_bd, row(rg_b_x), row(rg_lambda),
              ks, bs, ms, lp, p_re, p_im, row(s5_d),
              s5_w_glu.astype(bf16), row(s5_b_glu)]
    width = w_in.shape[1]
    nc = T // L
    return pl.pallas_call(
        _even_kernel,
        grid=(bsz, nt),
        in_specs=[pl.BlockSpec((T, dm), lambda b, t: (b * nt + t, 0))]
                 + [_const_spec(c.shape) for c in consts],
        out_specs=[pl.BlockSpec((T, RG_WIDTH), lambda b, t: (b * nt + t, 0)),
                   pl.BlockSpec((T, S5_WIDTH), lambda b, t: (b * nt + t, 0))],
        out_shape=[jax.ShapeDtypeStruct((n, RG_WIDTH), f32),
                   jax.ShapeDtypeStruct((n, S5_WIDTH), f32)],
        scratch_shapes=[pltpu.VMEM((T, width), f32),
                        pltpu.VMEM((RG_WIDTH // LANES, T + SUBLANES, LANES), f32),
                        pltpu.VMEM((T, RG_WIDTH), f32), pltpu.VMEM((T, RG_WIDTH), f32),
                        pltpu.VMEM((SUBLANES, RG_WIDTH), f32),
                        pltpu.VMEM((S5_WIDTH // LANES, T + SUBLANES, LANES), f32),
                        pltpu.VMEM((nc, ns), f32), pltpu.VMEM((nc, ns), f32),
                        pltpu.VMEM((SUBLANES, ns), f32), pltpu.VMEM((SUBLANES, ns), f32),
                        pltpu.VMEM((SUBLANES + nc, 2 * ns), f32),
                        pltpu.VMEM((S5_WIDTH // LANES, T, LANES), f32)],
        compiler_params=_params(("arbitrary", "arbitrary")),
        name="even_mixer",
    )(h2d, *consts)


HG_T = 512
HG_GROUP = 4
HG_UNROLL = 2


def _split3(x):
    hi = x.astype(bf16)
    r1 = x - hi.astype(f32)
    mid = r1.astype(bf16)
    lo = (r1 - mid.astype(f32)).astype(bf16)
    return hi, mid, lo


def _hgrn_kernel(p_ref, lb_ref, nw_ref, o_ref, st_ref):
    C = HG_CHUNK
    t = pl.program_id(1)

    @pl.when(t == 0)
    def _():
        st_ref[...] = jnp.zeros_like(st_ref)

    ri = lax.broadcasted_iota(jnp.int32, (C, C), 0)
    ci = lax.broadcasted_iota(jnp.int32, (C, C), 1)
    causal = ri >= ci
    tril = causal.astype(bf16)
    lb = lb_ref[...]

    heads = range(HG_HEADS)
    sls = [slice(h * HG_DK, (h + 1) * HG_DK) for h in heads]

    def group(gidx, carry):
        prep = []
        for ci_ in range(HG_GROUP):
            r0 = pl.multiple_of((gidx * HG_GROUP + ci_) * C, C)
            q = p_ref[pl.ds(r0, C), 0:HG_W]
            z = p_ref[pl.ds(r0, C), HG_W:2 * HG_W]
            sg = jax.nn.sigmoid(z)
            log_f = jnp.log(lb + (1.0 - lb) * sg)
            kk = (1.0 - lb) * (1.0 - sg)
            hi, mid_, lo = _split3(log_f)
            cum = _dot(tril, hi) + _dot(tril, mid_) + _dot(tril, lo)
            prep.append((r0, jax.nn.silu(q), kk, cum))
        mats = []
        for r0, qs, kk, cum in prep:
            midv = cum[C // 2:C // 2 + 1, :]
            last = cum[C - 1:C, :]
            qa = (qs * jnp.exp(cum - midv)).astype(bf16)
            kb = (kk * jnp.exp(midv - cum)).astype(bf16)
            qi = (qs * jnp.exp(cum)).astype(bf16)
            kc = (kk * jnp.exp(last - cum)).astype(bf16)
            v = [p_ref[pl.ds(r0, C), 2 * HG_W + h * HG_DK:2 * HG_W + (h + 1) * HG_DK].astype(bf16)
                 for h in heads]
            sc = [_dot_nt(qa[:, sl], kb[:, sl]) for sl in sls]
            con_t = [_dot_tn(v[h], kc[:, sl]) for h, sl in enumerate(sls)]
            mats.append((r0, qi, v, sc, con_t, jnp.exp(last)))
        st = [st_ref[h] for h in heads]
        outs = []
        for r0, qi, v, sc, con_t, dec in mats:
            oi = [_dot_nt(qi[:, sl], st[h].astype(bf16)) for h, sl in enumerate(sls)]
            st = [dec[:, sl] * st[h] + con_t[h] for h, sl in enumerate(sls)]
            outs.append((r0, v, sc, oi))
        for h in heads:
            st_ref[h] = st[h]
        for r0, v, sc, oi in outs:
            o = [_dot(jnp.where(causal, sc[h], 0.0).astype(bf16), v[h]) + oi[h] for h in heads]
            for h, sl in enumerate(sls):
                g_h = p_ref[pl.ds(r0, C), 3 * HG_W + h * HG_DK:3 * HG_W + (h + 1) * HG_DK]
                on = o[h] * lax.rsqrt(jnp.mean(o[h] * o[h], axis=-1, keepdims=True) + EPS) * nw_ref[...]
                o_ref[pl.ds(r0, C), sl] = on * jax.nn.silu(g_h)
        return carry

    lax.fori_loop(0, HG_T // (C * HG_GROUP), group, 0, unroll=HG_UNROLL)


def _hgrn_mixer(hg, bsz, seq, lb, norm_w):
    n = hg.shape[0]
    T = HG_T
    nt = seq // T
    return pl.pallas_call(
        _hgrn_kernel,
        grid=(bsz, nt),
        in_specs=[pl.BlockSpec((T, 4 * HG_W), lambda b, t: (b * nt + t, 0)),
                  _const_spec((1, HG_W)), _const_spec((1, HG_DK))],
        out_specs=pl.BlockSpec((T, HG_W), lambda b, t: (b * nt + t, 0)),
        out_shape=jax.ShapeDtypeStruct((n, HG_W), f32),
        scratch_shapes=[pltpu.VMEM((HG_HEADS, HG_DK, HG_DK), f32)],
        compiler_params=_params(("arbitrary", "arbitrary")),
        name="hgrn_mixer",
    )(hg, lb.reshape(1, HG_W).astype(f32), norm_w.reshape(1, HG_DK).astype(f32))


N_PAIR = DA_WIDTH // LANES
DA_MERGE_ROWS = 256


def _attn_group(gi, d, n, sd_ref, x_ref, qd, kv, og, lg, bias_scr):
    B = DA_BLOCK
    L = DA_SPAN // d
    nblk = L // B
    nblk_log = int(math.log2(nblk))
    stride_r = L + B

    @pl.when(n == 0)
    def _():
        for r in range(d):
            kv[:, r * stride_r:r * stride_r + B, :] = jnp.zeros((2 * N_PAIR, B, LANES), bf16)

    scale = DA_HEAD_DIM ** -0.5
    for r in range(d):
        rows = pl.ds(r, L, stride=d) if d > 1 else pl.ds(0, L)
        for s in range(N_PAIR):
            qd[s, r * L:(r + 1) * L, :] = (x_ref[s, rows, :] * scale).astype(bf16)
        for s in range(2 * N_PAIR):
            kv[s, r * stride_r + B:r * stride_r + B + L, :] = x_ref[N_PAIR + s, rows, :].astype(bf16)

    qi = lax.broadcasted_iota(jnp.int32, (B, 2 * B), 0)
    kj = lax.broadcasted_iota(jnp.int32, (B, 2 * B), 1)
    rel = qi + B - kj
    valid = (rel >= 0) & (rel <= B)
    relf = rel.astype(f32)
    for h in range(DA_HEADS):
        bias = jnp.where(valid, -sd_ref[gi, h] * relf, NEG_INF)
        bias_scr[0, h] = bias
        bias_scr[1, h] = jnp.where(kj < B, NEG_INF, bias)

    lane = lax.broadcasted_iota(jnp.int32, (B, LANES), 1)
    lo_half = lane < DA_HEAD_DIM

    def blk_body(blk, carry):
        r = blk >> nblk_log
        m = blk & (nblk - 1)
        q0 = pl.multiple_of(blk * B, B)
        k0 = pl.multiple_of((blk + r) * B, B)
        first = jnp.logical_and(n == 0, m == 0).astype(jnp.int32)
        start = m * (B * d) + r
        orow = pl.ds(start, B, stride=d) if d > 1 else pl.ds(pl.multiple_of(start, B), B)
        for j in range(N_PAIR):
            qp = qd[j, pl.ds(q0, B), :]
            kp = kv[j, pl.ds(k0, 2 * B), :]
            vp = kv[N_PAIR + j, pl.ds(k0, 2 * B), :]
            outs, lses = [], []
            for e in range(2):
                keep = lo_half if e == 0 else jnp.logical_not(lo_half)
                qm = jnp.where(keep, qp, jnp.zeros_like(qp))
                s = _dot_nt(qm, kp) + bias_scr[first, 2 * j + e]
                mx = jnp.max(s, axis=-1, keepdims=True)
                ex = jnp.exp(s - mx)
                l = jnp.sum(ex, axis=-1, keepdims=True)
                outs.append(_dot(ex.astype(bf16), vp) * (1.0 / l))
                lses.append(jnp.broadcast_to(mx + jnp.log(l), (B, LANES)))
            og[gi, j, orow, :] = jnp.where(lo_half, outs[0], outs[1])
            lg[gi, j, orow, :] = jnp.where(lo_half, lses[0], lses[1])
        return carry

    lax.fori_loop(0, DA_SPAN // B, blk_body, 0, unroll=16)

    for r in range(d):
        kv[:, r * stride_r:r * stride_r + B, :] = kv[:, r * stride_r + L:r * stride_r + L + B, :]


def _attn_kernel(sd_ref, x_ref, o_ref, qd, kv0, kv1, kv2, og, lg, bias_scr):
    n = pl.program_id(1)
    g = pl.program_id(2)
    kvs = (kv0, kv1, kv2)
    for gi, (_, d) in enumerate(DA_PATTERNS):
        @pl.when(g == gi)
        def _(gi=gi, d=d):
            _attn_group(gi, d, n, sd_ref, x_ref, qd, kvs[gi], og, lg, bias_scr)

    @pl.when(g == len(DA_PATTERNS) - 1)
    def _():
        R = DA_MERGE_ROWS

        def merge(i, carry):
            r0 = pl.multiple_of(i * R, R)
            for j in range(N_PAIR):
                ls = [lg[gi, j, pl.ds(r0, R), :] for gi in range(len(DA_PATTERNS))]
                mx = functools.reduce(jnp.maximum, ls)
                ws = [jnp.exp(l - mx) for l in ls]
                den = functools.reduce(lambda a, b: a + b, ws)
                num = functools.reduce(lambda a, b: a + b,
                                       [w * og[gi, j, pl.ds(r0, R), :] for gi, w in enumerate(ws)])
                o_ref[pl.ds(r0, R), j * LANES:(j + 1) * LANES] = num * (1.0 / den)
            return carry

        lax.fori_loop(0, DA_SPAN // R, merge, 0)


def _attn_mixer(qkv_slabs, bsz, seq, sd):
    ng = len(DA_PATTERNS)
    n = qkv_slabs.shape[1]
    nt = seq // DA_SPAN
    per_g = 3 * N_PAIR
    kv_scratch = [pltpu.VMEM((2 * N_PAIR, DA_SPAN + DA_BLOCK * d, LANES), bf16) for _, d in DA_PATTERNS]
    return pl.pallas_call(
        _attn_kernel,
        grid=(bsz, nt, ng),
        in_specs=[pl.BlockSpec(memory_space=pltpu.SMEM),
                  pl.BlockSpec((per_g, DA_SPAN, LANES), lambda b, t, g: (g, b * nt + t, 0))],
        out_specs=pl.BlockSpec((DA_SPAN, DA_WIDTH), lambda b, t, g: (b * nt + t, 0)),
        out_shape=jax.ShapeDtypeStruct((n, DA_WIDTH), f32),
        scratch_shapes=[pltpu.VMEM((N_PAIR, DA_SPAN, LANES), bf16)] + kv_scratch + [
            pltpu.VMEM((ng, N_PAIR, DA_SPAN, LANES), f32),
            pltpu.VMEM((ng, N_PAIR, DA_SPAN, LANES), f32),
            pltpu.VMEM((2, DA_HEADS, DA_BLOCK, 2 * DA_BLOCK), f32)],
        compiler_params=_params(("arbitrary", "arbitrary", "arbitrary")),
        name="dilated_attn",
    )(sd, qkv_slabs)


POST_SUB = 256
POST_NSUB = 2
POST_TM = POST_SUB * POST_NSUB
FF_CHUNK = 256


def _post_kernel(ya_ref, yb_ref, h_ref, woa_ref, wob_ref, g_ref, wup_ref, cw_ref, cb_ref, wdn_ref,
                 o_ref, xn_scr, st_scr, carry_scr, acc_scr, act_scr):
    tm = POST_SUB
    C = FF_CHUNK
    dff = wdn_ref.shape[0]
    kc = cw_ref.shape[0]
    t = pl.program_id(1)
    nchunk = dff // C
    ns = C // LANES

    @pl.when(t == 0)
    def _():
        carry_scr[...] = jnp.zeros_like(carry_scr)

    def prologue(i):
        rows = pl.ds(i * tm, tm)
        y1 = (_dot(ya_ref[rows, :].astype(bf16), woa_ref[...])
              + _dot(yb_ref[rows, :].astype(bf16), wob_ref[...]))
        h1 = h_ref[rows, :] + _rms(y1, g_ref[1:2, :])
        o_ref[rows, :] = h1
        xn_scr[i] = _rms(h1, g_ref[2:3, :]).astype(bf16)
        acc_scr[i] = jnp.zeros(acc_scr.shape[1:], f32)

    def up(i, j):
        slot = j % 2
        xn = xn_scr[i]
        c0 = j * C
        rv = _dot(xn, wup_ref[:, c0:c0 + C])
        rg = _dot(xn, wup_ref[:, dff + c0:dff + c0 + C])
        st_scr[slot, :, 0:SUBLANES, :] = carry_scr[j]
        for s in range(ns):
            st_scr[slot, s, SUBLANES:SUBLANES + tm, :] = rv[:, s * LANES:(s + 1) * LANES]
            st_scr[slot, ns + s, SUBLANES:SUBLANES + tm, :] = rg[:, s * LANES:(s + 1) * LANES]

    def act(j):
        slot = j % 2
        parts = []
        for s in range(ns):
            c0 = j * C + s * LANES
            cv = cb_ref[:, c0:c0 + LANES]
            cg = cb_ref[:, dff + c0:dff + c0 + LANES]
            for k in range(kc):
                off = SUBLANES - (kc - 1) + k
                cv = cv + cw_ref[k:k + 1, c0:c0 + LANES] * st_scr[slot, s, off:off + tm, :]
                cg = cg + cw_ref[k:k + 1, dff + c0:dff + c0 + LANES] * st_scr[slot, ns + s, off:off + tm, :]
            parts.append((jax.nn.gelu(cg) * cv).astype(bf16))
        act_scr[slot] = jnp.concatenate(parts, axis=1)
        carry_scr[j] = st_scr[slot, :, tm:tm + SUBLANES, :]

    def down(i, j):
        acc_scr[i] += _dot(act_scr[j % 2], wdn_ref[j * C:(j + 1) * C, :])

    def ffn(i):
        up(i, 0)
        for j in range(nchunk):
            if j + 1 < nchunk:
                up(i, j + 1)
            if j >= 1:
                down(i, j - 1)
            act(j)
        down(i, nchunk - 1)

    def epilogue(i):
        rows = pl.ds(i * tm, tm)
        o_ref[rows, :] = o_ref[rows, :] + _rms(acc_scr[i], g_ref[3:4, :])

    for i in range(POST_NSUB):
        prologue(i)
    for i in range(POST_NSUB):
        ffn(i)
        epilogue(i)


def _layer_spec(shape, layer):
    nd = len(shape) - 1
    return pl.BlockSpec((None,) + tuple(shape[1:]), lambda *_: (layer,) + (0,) * nd,
                        pipeline_mode=pl.Buffered(1))


def _post(ya, yb, h2d, bsz, seq, w_out, layer, g_all, w_up_all, conv_w_all, conv_b_all, w_down_all):
    n, d = h2d.shape
    tm = POST_TM
    nt = seq // tm
    wa = ya.shape[1]
    wb = yb.shape[1]
    dff = w_down_all.shape[1]
    woa = w_out[:wa].astype(bf16)
    wob = w_out[wa:].astype(bf16)
    stacked = [g_all, w_up_all, conv_w_all, conv_b_all, w_down_all]
    consts = [woa, wob] + stacked
    row_spec = lambda w: pl.BlockSpec((tm, w), lambda b, t: (b * nt + t, 0))
    return pl.pallas_call(
        _post_kernel,
        grid=(bsz, nt),
        in_specs=[row_spec(wa), row_spec(wb), row_spec(d), _const_spec(woa.shape), _const_spec(wob.shape)]
                 + [_layer_spec(c.shape, layer) for c in stacked],
        out_specs=row_spec(d),
        out_shape=jax.ShapeDtypeStruct((n, d), f32),
        scratch_shapes=[pltpu.VMEM((POST_NSUB, POST_SUB, d), bf16),
                        pltpu.VMEM((2, 2 * FF_CHUNK // LANES, POST_SUB + SUBLANES, LANES), f32),
                        pltpu.VMEM((dff // FF_CHUNK, 2 * FF_CHUNK // LANES, SUBLANES, LANES), f32),
                        pltpu.VMEM((POST_NSUB, POST_SUB, d), f32),
                        pltpu.VMEM((2, POST_SUB, FF_CHUNK), bf16)],
        compiler_params=_params(("arbitrary", "arbitrary")),
        name="post_ffn",
    )(ya, yb, h2d, *consts)


def kernel(x, norm_g, ffn_w_up, ffn_conv_w, ffn_conv_b, ffn_w_down, ev_w_in, ev_w_out, rg_conv_w, rg_conv_b,
           rg_w_a, rg_b_a, rg_w_x, rg_b_x, rg_lambda, s5_a_re, s5_a_im, s5_b_re, s5_b_im, s5_c_re, s5_c_im,
           s5_d, s5_log_dt, s5_w_glu, s5_b_glu, od_w_in, od_w_out, hg_lower, hg_norm_g):
    bsz, seq, d = x.shape
    depth = norm_g.shape[0]
    assert seq % DA_SPAN == 0 and d % LANES == 0
    lb_p = jax.nn.softmax(hg_lower.astype(f32), axis=0)
    lb_all = jnp.cumsum(lb_p, axis=0) - lb_p[0]
    ngrp = len(DA_PATTERNS)
    slopes = (2.0 ** (-8.0 * jnp.arange(1, ngrp * DA_HEADS + 1, dtype=f32) / (ngrp * DA_HEADS))
              ).reshape(ngrp, DA_HEADS)
    sd = slopes * jnp.asarray([float(p[1]) for p in DA_PATTERNS], f32)[:, None]

    g_all = norm_g.astype(f32)
    w_up_all = ffn_w_up.astype(bf16)
    w_down_all = ffn_w_down.astype(bf16)
    conv_w_all = ffn_conv_w.astype(f32)
    conv_b_all = ffn_conv_b.astype(f32).reshape(depth, 1, -1)

    h = x.reshape(bsz * seq, d)
    for layer in range(depth):
        j = layer // 2
        g = norm_g[layer]
        if layer % 2 == 0:
            ya, yb = _even_mixer(h, g[0], ev_w_in[j], bsz, seq, rg_conv_w[j], rg_conv_b[j], rg_w_a[j], rg_b_a[j],
                                 rg_w_x[j], rg_b_x[j], rg_lambda[j], s5_a_re[j], s5_a_im[j],
                                 s5_b_re[j], s5_b_im[j], s5_c_re[j], s5_c_im[j], s5_d[j], s5_log_dt[j],
                                 s5_w_glu[j], s5_b_glu[j])
            w_out = ev_w_out[j]
        else:
            hgp, qkv = _norm_proj(h, g[0], od_w_in[j].astype(bf16),
                                  (("flat", 4 * HG_W), ("slab", ngrp * 3 * DA_WIDTH)))
            ya = _hgrn_mixer(hgp, bsz, seq, lb_all[layer], hg_norm_g[j])
            yb = _attn_mixer(qkv, bsz, seq, sd)
            w_out = od_w_out[j]
        h = _post(ya, yb, h, bsz, seq, w_out, layer, g_all, w_up_all, conv_w_all, conv_b_all, w_down_all)
    return h.reshape(bsz, seq, d)
```

```python
import functools
import math

import jax
import jax.numpy as jnp
from jax import lax
from jax.experimental import pallas as pl
from jax.experimental.pallas import tpu as pltpu

f32 = jnp.float32
bf16 = jnp.bfloat16

EPS = 1e-6
NEG_INF = -1e30
RG_C = 8.0
RG_HEADS = 8
RG_WIDTH = 512
S5_WIDTH = 256
S5_GROUP = 16
S5_STATE = 64
HG_HEADS = 4
HG_DK = 128
HG_CHUNK = 64
HG_W = HG_HEADS * HG_DK
DA_HEADS = 4
DA_HEAD_DIM = 64
DA_WIDTH = DA_HEADS * DA_HEAD_DIM
DA_PATTERNS = ((128, 1), (512, 4), (2048, 16))
DA_BLOCK = 128
DA_SPAN = 2048
LANES = 128
SUBLANES = 8
VMEM_LIMIT = 56 * 1024 * 1024


def _rms(x, g):
    return x * lax.rsqrt(jnp.mean(x * x, axis=-1, keepdims=True) + EPS) * g


def _dot(a, b):
    return jnp.dot(a, b, preferred_element_type=f32)


def _dot_nt(a, b):
    return lax.dot_general(a, b, (((1,), (1,)), ((), ())), preferred_element_type=f32)


def _dot_tn(a, b):
    return lax.dot_general(a, b, (((0,), (0,)), ((), ())), preferred_element_type=f32)


def _const_spec(shape):
    nd = len(shape)
    return pl.BlockSpec(shape, lambda *_: (0,) * nd, pipeline_mode=pl.Buffered(1))


def _params(sem):
    return pltpu.CompilerParams(dimension_semantics=sem, vmem_limit_bytes=VMEM_LIMIT)


PROJ_TM = 512
PROJ_CHUNK = 512


def _norm_proj_kernel(x_ref, g_ref, w_ref, *out_refs, layout):
    xn = _rms(x_ref[...], g_ref[...]).astype(bf16)
    col = 0
    for o_ref, (kind, width) in zip(out_refs, layout):
        for c0 in range(0, width, PROJ_CHUNK):
            cw = min(PROJ_CHUNK, width - c0)
            res = _dot(xn, w_ref[:, col + c0:col + c0 + cw])
            if kind == "flat":
                o_ref[:, c0:c0 + cw] = res
            else:
                for s in range(cw // LANES):
                    o_ref[c0 // LANES + s] = res[:, s * LANES:(s + 1) * LANES]
        col += width


def _norm_proj(h2d, g, w, layout):
    n, d = h2d.shape
    tm = PROJ_TM
    out_shape, out_specs = [], []
    for kind, width in layout:
        if kind == "flat":
            out_shape.append(jax.ShapeDtypeStruct((n, width), f32))
            out_specs.append(pl.BlockSpec((tm, width), lambda i: (i, 0)))
        else:
            out_shape.append(jax.ShapeDtypeStruct((width // LANES, n, LANES), f32))
            out_specs.append(pl.BlockSpec((width // LANES, tm, LANES), lambda i: (0, i, 0)))
    return pl.pallas_call(
        functools.partial(_norm_proj_kernel, layout=layout),
        grid=(n // tm,),
        in_specs=[pl.BlockSpec((tm, d), lambda i: (i, 0)),
                  _const_spec((1, d)),
                  _const_spec(w.shape)],
        out_specs=out_specs,
        out_shape=out_shape,
        compiler_params=_params(("arbitrary",)),
        name="norm_proj",
    )(h2d, g.reshape(1, d), w)


EV_T = 512
S5_L = 4
EV_PROJ_CHUNK = 256


def _even_kernel(h_ref, g_ref, win_ref, rcw_ref, rcb_ref, wa_ref, ba_ref, wx_ref, bx_ref, lam_ref,
                 ks_ref, bs_ref, ms_ref, lp_ref, pre_ref, pim_ref,
                 d5_ref, wg_ref, bg_ref,
                 ya_ref, yb_ref,
                 pj, xst, rga, rgb, rgc, ust, sre, sim, scr, sci, hst, yst):
    T = EV_T
    t = pl.program_id(1)

    @pl.when(t == 0)
    def _():
        xst[:, 0:SUBLANES, :] = jnp.zeros((RG_WIDTH // LANES, SUBLANES, LANES), f32)
        ust[:, 0:SUBLANES, :] = jnp.zeros((S5_WIDTH // LANES, SUBLANES, LANES), f32)
        rgc[...] = jnp.zeros_like(rgc)
        scr[...] = jnp.zeros_like(scr)
        sci[...] = jnp.zeros_like(sci)

    xn = _rms(h_ref[...], g_ref[...]).astype(bf16)
    for c0 in range(0, pj.shape[1], EV_PROJ_CHUNK):
        pj[:, c0:c0 + EV_PROJ_CHUNK] = _dot(xn, win_ref[:, c0:c0 + EV_PROJ_CHUNK])
    p_ref = pj

    kc = rcw_ref.shape[0]
    us = []
    for s in range(RG_WIDTH // LANES):
        cols = slice(s * LANES, (s + 1) * LANES)
        xst[s, SUBLANES:SUBLANES + T, :] = p_ref[:, cols]
        us_ = rcb_ref[:, cols]
        for k in range(kc):
            off = SUBLANES - (kc - 1) + k
            us_ = us_ + rcw_ref[k:k + 1, cols] * xst[s, off:off + T, :]
        us.append(us_)
    u = jnp.concatenate(us, axis=1)
    xst[:, 0:SUBLANES, :] = xst[:, T:T + SUBLANES, :]
    ub = u.astype(bf16)
    r = jax.nn.sigmoid(_dot(ub, wa_ref[...]) + ba_ref[...])
    ig = jax.nn.sigmoid(_dot(ub, wx_ref[...]) + bx_ref[...])
    log_a = (RG_C * r) * jax.nn.log_sigmoid(lam_ref[...])
    a = jnp.exp(log_a)
    bt = jnp.sqrt(-jnp.tanh(log_a) * (a * a + 1.0)) * (ig * u)
    ngrp = T // SUBLANES
    a = a.reshape(ngrp, SUBLANES, RG_WIDTH)
    bt = bt.reshape(ngrp, SUBLANES, RG_WIDTH)
    rowm = lax.broadcasted_iota(jnp.int32, (1, SUBLANES, RG_WIDTH), 1)
    for s in (1, 2, 4):
        m = rowm >= s
        bt = bt + jnp.where(m, a * pltpu.roll(bt, s, 1), 0.0)
        a = a * jnp.where(m, pltpu.roll(a, s, 1), 1.0)
    rga[...] = a.reshape(T, RG_WIDTH)
    rgb[...] = bt.reshape(T, RG_WIDTH)
    ya_ref[...] = jax.nn.gelu(p_ref[:, RG_WIDTH:2 * RG_WIDTH])

    L = S5_L
    nc = T // L
    ns = sre.shape[1]
    nsl = S5_WIDTH // LANES
    u5 = p_ref[:, 2 * RG_WIDTH:2 * RG_WIDTH + S5_WIDTH]
    for s in range(nsl):
        ust[s, SUBLANES:SUBLANES + T, :] = u5[:, s * LANES:(s + 1) * LANES]
    rowl = lax.broadcasted_iota(jnp.int32, (T, LANES), 0) & (L - 1)
    pieces, ends = [], []
    for k in range(L):
        for s in range(nsl):
            p = ust[s, SUBLANES - k:SUBLANES - k + T, :]
            if k:
                p = jnp.where(rowl >= k, p, 0.0)
            pieces.append(p.astype(bf16))
            ends.append(ust[s, pl.ds(SUBLANES + L - 1 - k, nc, stride=L), :].astype(bf16))
    yb_ref[...] = _dot(jnp.concatenate(pieces, axis=1), ks_ref[...]) + d5_ref[...] * u5
    gend = _dot(jnp.concatenate(ends, axis=1), bs_ref[...])
    ngrp5 = nc // SUBLANES
    hre = gend[:, 0:ns].reshape(ngrp5, SUBLANES, ns)
    him = gend[:, ns:2 * ns].reshape(ngrp5, SUBLANES, ns)
    for i, s in enumerate((1, 2, 4)):
        cr = lp_ref[2 * i][None]
        ci = lp_ref[2 * i + 1][None]
        zr = pltpu.roll(hre, s, 1)
        zi = pltpu.roll(him, s, 1)
        hre, him = hre + (cr * zr - ci * zi), him + (cr * zi + ci * zr)
    sre[...] = hre.reshape(nc, ns)
    sim[...] = him.reshape(nc, ns)
    hst[0:SUBLANES, 0:ns] = scr[...]
    hst[0:SUBLANES, ns:2 * ns] = sci[...]

    def rg_body(gidx, c):
        r0 = gidx * SUBLANES
        h = rga[pl.ds(r0, SUBLANES), :] * c + rgb[pl.ds(r0, SUBLANES), :]
        rgb[pl.ds(r0, SUBLANES), :] = h
        return jnp.broadcast_to(h[SUBLANES - 1:SUBLANES, :], (SUBLANES, RG_WIDTH))

    c = rgc[...]
    for gidx in range(T // SUBLANES):
        c = rg_body(gidx, c)
    rgc[...] = c

    def s5_body(gidx, c):
        cre, cim = c
        r0 = gidx * SUBLANES
        pr = pre_ref[...]
        pi = pim_ref[...]
        hr = sre[pl.ds(r0, SUBLANES), :] + (pr * cre - pi * cim)
        hi = sim[pl.ds(r0, SUBLANES), :] + (pr * cim + pi * cre)
        sre[pl.ds(r0, SUBLANES), :] = hr
        sim[pl.ds(r0, SUBLANES), :] = hi
        return (jnp.broadcast_to(hr[SUBLANES - 1:SUBLANES, :], (SUBLANES, ns)),
                jnp.broadcast_to(hi[SUBLANES - 1:SUBLANES, :], (SUBLANES, ns)))

    cc = (scr[...], sci[...])
    for gidx in range(ngrp5):
        cc = s5_body(gidx, cc)
    cre, cim = cc
    scr[...] = cre
    sci[...] = cim
    hst[SUBLANES:SUBLANES + nc, 0:ns] = sre[...]
    hst[SUBLANES:SUBLANES + nc, ns:2 * ns] = sim[...]
    hprev = hst[SUBLANES - 1:SUBLANES - 1 + nc, :].astype(bf16)
    yi = _dot(hprev, ms_ref[...])
    for r in range(L):
        for s in range(nsl):
            c0 = r * S5_WIDTH + s * LANES
            yst[s, pl.ds(r, nc, stride=L), :] = yi[:, c0:c0 + LANES]
    ya_ref[...] = ya_ref[...] * rgb[...]
    y5 = yb_ref[...] + jnp.concatenate([yst[s] for s in range(nsl)], axis=1)
    v = jax.nn.gelu(y5)
    yb_ref[...] = v * jax.nn.sigmoid(_dot(v.astype(bf16), wg_ref[...]) + bg_ref[...])


def _block_diag(w):
    *lead, g, a, b = w.shape
    w2 = w.reshape(*lead, g * a, b).astype(bf16)
    rep = (jnp.arange(g * b)[None, :] % b == jnp.arange(b)[:, None]).astype(bf16)
    wide = jnp.einsum('...rb,bc->...rc', w2, rep, preferred_element_type=f32)
    own = (jnp.arange(g * a)[:, None] // a) == (jnp.arange(g * b)[None, :] // b)
    return jnp.where(own, wide, 0.0).astype(bf16)


def _even_mixer(h2d, g0, w_in, bsz, seq, rg_conv_w, rg_conv_b, rg_w_a, rg_b_a, rg_w_x, rg_b_x, rg_lambda,
                s5_a_re, s5_a_im, s5_b_re, s5_b_im, s5_c_re, s5_c_im, s5_d, s5_log_dt,
                s5_w_glu, s5_b_glu):
    n, dm = h2d.shape
    T = EV_T
    L = S5_L
    nt = seq // T
    ns = s5_a_re.shape[0] * S5_STATE
    dt = jnp.exp(s5_log_dt.astype(f32))[:, None]
    ar = s5_a_re.astype(f32)
    ai = s5_a_im.astype(f32)
    mag = jnp.exp(ar * dt)
    abar_re = mag * jnp.cos(ai * dt)
    abar_im = mag * jnp.sin(ai * dt)
    den = ar * ar + ai * ai
    num_re = abar_re - 1.0
    f_re = (num_re * ar + abar_im * ai) / den
    f_im = (abar_im * ar - num_re * ai) / den
    br = s5_b_re.astype(f32)
    bi = s5_b_im.astype(f32)
    bb_re = f_re[..., None] * br - f_im[..., None] * bi
    bb_im = f_re[..., None] * bi + f_im[..., None] * br
    cr = s5_c_re.astype(f32)
    ci = s5_c_im.astype(f32)
    jj = jnp.arange(SUBLANES * L + 1, dtype=f32)[:, None, None]
    pmag = jnp.exp(jj * (ar * dt)[None])
    pang = jj * (ai * dt)[None]
    pwr = pmag * jnp.cos(pang)
    pwi = pmag * jnp.sin(pang)
    lr, li = pwr[:L, :, None, :], pwi[:L, :, None, :]
    clr = cr[None] * lr - ci[None] * li
    cli = cr[None] * li + ci[None] * lr
    kk = jnp.einsum('kgpn,gnq->kgqp', clr, bb_re) - jnp.einsum('kgpn,gnq->kgqp', cli, bb_im)
    ks = _block_diag(kk).reshape(L * S5_WIDTH, S5_WIDTH).astype(bf16)
    lbr = pwr[:L, :, :, None] * bb_re[None] - pwi[:L, :, :, None] * bb_im[None]
    lbi = pwr[:L, :, :, None] * bb_im[None] + pwi[:L, :, :, None] * bb_re[None]
    bs = jnp.concatenate([_block_diag(jnp.swapaxes(lbr, 2, 3)), _block_diag(jnp.swapaxes(lbi, 2, 3))],
                         axis=2).reshape(L * S5_WIDTH, 2 * ns).astype(bf16)
    l1r, l1i = pwr[1:L + 1, :, None, :], pwi[1:L + 1, :, None, :]
    c1r = cr[None] * l1r - ci[None] * l1i
    c1i = cr[None] * l1i + ci[None] * l1r
    ms = jnp.concatenate([_block_diag(jnp.swapaxes(c1r, 2, 3)), _block_diag(-jnp.swapaxes(c1i, 2, 3))], axis=1)
    ms = jnp.swapaxes(ms, 0, 1).reshape(2 * ns, L * S5_WIDTH).astype(bf16)
    shifts = jnp.asarray((1, 2, 4))
    rows = jnp.arange(SUBLANES)[None, :, None]
    keep = rows >= shifts[:, None, None]
    lp = jnp.stack([jnp.where(keep, p[L * shifts].reshape(3, 1, ns), 0.0) for p in (pwr, pwi)],
                   axis=1).reshape(6, SUBLANES, ns)
    p_re = pwr[L::L].reshape(SUBLANES, ns)
    p_im = pwi[L::L].reshape(SUBLANES, ns)
    wa_bd = _block_diag(rg_w_a).astype(bf16)
    wx_bd = _block_diag(rg_w_x).astype(bf16)
    row = lambda v: v.reshape(1, -1).astype(f32)
    consts = [row(g0), w_in.astype(bf16),
              rg_conv_w.astype(f32), row(rg_conv_b), wa_bd, row(rg_b_a), wx_bd, row(rg_b_x), row(rg_lambda),
              ks, bs, ms, lp, p_re, p_im, row(s5_d),
              s5_w_glu.astype(bf16), row(s5_b_glu)]
    width = w_in.shape[1]
    nc = T // L
    return pl.pallas_call(
        _even_kernel,
        grid=(bsz, nt),
        in_specs=[pl.BlockSpec((T, dm), lambda b, t: (b * nt + t, 0))]
                 + [_const_spec(c.shape) for c in consts],
        out_specs=[pl.BlockSpec((T, RG_WIDTH), lambda b, t: (b * nt + t, 0)),
                   pl.BlockSpec((T, S5_WIDTH), lambda b, t: (b * nt + t, 0))],
        out_shape=[jax.ShapeDtypeStruct((n, RG_WIDTH), f32),
                   jax.ShapeDtypeStruct((n, S5_WIDTH), f32)],
        scratch_shapes=[pltpu.VMEM((T, width), f32),
                        pltpu.VMEM((RG_WIDTH // LANES, T + SUBLANES, LANES), f32),
                        pltpu.VMEM((T, RG_WIDTH), f32), pltpu.VMEM((T, RG_WIDTH), f32),
                        pltpu.VMEM((SUBLANES, RG_WIDTH), f32),
                        pltpu.VMEM((S5_WIDTH // LANES, T + SUBLANES, LANES), f32),
                        pltpu.VMEM((nc, ns), f32), pltpu.VMEM((nc, ns), f32),
                        pltpu.VMEM((SUBLANES, ns), f32), pltpu.VMEM((SUBLANES, ns), f32),
                        pltpu.VMEM((SUBLANES + nc, 2 * ns), f32),
                        pltpu.VMEM((S5_WIDTH // LANES, T, LANES), f32)],
        compiler_params=_params(("arbitrary", "arbitrary")),
        name="even_mixer",
    )(h2d, *consts)


HG_T = 512
HG_GROUP = 4
HG_UNROLL = 2


def _split3(x):
    hi = x.astype(bf16)
    r1 = x - hi.astype(f32)
    mid = r1.astype(bf16)
    lo = (r1 - mid.astype(f32)).astype(bf16)
    return hi, mid, lo


def _hgrn_kernel(p_ref, lb_ref, nw_ref, o_ref, st_ref):
    C = HG_CHUNK
    t = pl.program_id(1)

    @pl.when(t == 0)
    def _():
        st_ref[...] = jnp.zeros_like(st_ref)

    ri = lax.broadcasted_iota(jnp.int32, (C, C), 0)
    ci = lax.broadcasted_iota(jnp.int32, (C, C), 1)
    causal = ri >= ci
    tril = causal.astype(bf16)
    lb = lb_ref[...]

    heads = range(HG_HEADS)
    sls = [slice(h * HG_DK, (h + 1) * HG_DK) for h in heads]

    def group(gidx, carry):
        prep = []
        for ci_ in range(HG_GROUP):
            r0 = pl.multiple_of((gidx * HG_GROUP + ci_) * C, C)
            q = p_ref[pl.ds(r0, C), 0:HG_W]
            z = p_ref[pl.ds(r0, C), HG_W:2 * HG_W]
            sg = jax.nn.sigmoid(z)
            log_f = jnp.log(lb + (1.0 - lb) * sg)
            kk = (1.0 - lb) * (1.0 - sg)
            hi, mid_, lo = _split3(log_f)
            cum = _dot(tril, hi) + _dot(tril, mid_) + _dot(tril, lo)
            prep.append((r0, jax.nn.silu(q), kk, cum))
        mats = []
        for r0, qs, kk, cum in prep:
            midv = cum[C // 2:C // 2 + 1, :]
            last = cum[C - 1:C, :]
            qa = (qs * jnp.exp(cum - midv)).astype(bf16)
            kb = (kk * jnp.exp(midv - cum)).astype(bf16)
            qi = (qs * jnp.exp(cum)).astype(bf16)
            kc = (kk * jnp.exp(last - cum)).astype(bf16)
            v = [p_ref[pl.ds(r0, C), 2 * HG_W + h * HG_DK:2 * HG_W + (h + 1) * HG_DK].astype(bf16)
                 for h in heads]
            sc = [_dot_nt(qa[:, sl], kb[:, sl]) for sl in sls]
            con_t = [_dot_tn(v[h], kc[:, sl]) for h, sl in enumerate(sls)]
            mats.append((r0, qi, v, sc, con_t, jnp.exp(last)))
        st = [st_ref[h] for h in heads]
        outs = []
        for r0, qi, v, sc, con_t, dec in mats:
            oi = [_dot_nt(qi[:, sl], st[h].astype(bf16)) for h, sl in enumerate(sls)]
            st = [dec[:, sl] * st[h] + con_t[h] for h, sl in enumerate(sls)]
            outs.append((r0, v, sc, oi))
        for h in heads:
            st_ref[h] = st[h]
        for r0, v, sc, oi in outs:
            o = [_dot(jnp.where(causal, sc[h], 0.0).astype(bf16), v[h]) + oi[h] for h in heads]
            for h, sl in enumerate(sls):
                g_h = p_ref[pl.ds(r0, C), 3 * HG_W + h * HG_DK:3 * HG_W + (h + 1) * HG_DK]
                on = o[h] * lax.rsqrt(jnp.mean(o[h] * o[h], axis=-1, keepdims=True) + EPS) * nw_ref[...]
                o_ref[pl.ds(r0, C), sl] = on * jax.nn.silu(g_h)
        return carry

    lax.fori_loop(0, HG_T // (C * HG_GROUP), group, 0, unroll=HG_UNROLL)


def _hgrn_mixer(hg, bsz, seq, lb, norm_w):
    n = hg.shape[0]
    T = HG_T
    nt = seq // T
    return pl.pallas_call(
        _hgrn_kernel,
        grid=(bsz, nt),
        in_specs=[pl.BlockSpec((T, 4 * HG_W), lambda b, t: (b * nt + t, 0)),
                  _const_spec((1, HG_W)), _const_spec((1, HG_DK))],
        out_specs=pl.BlockSpec((T, HG_W), lambda b, t: (b * nt + t, 0)),
        out_shape=jax.ShapeDtypeStruct((n, HG_W), f32),
        scratch_shapes=[pltpu.VMEM((HG_HEADS, HG_DK, HG_DK), f32)],
        compiler_params=_params(("arbitrary", "arbitrary")),
        name="hgrn_mixer",
    )(hg, lb.reshape(1, HG_W).astype(f32), norm_w.reshape(1, HG_DK).astype(f32))


N_PAIR = DA_WIDTH // LANES
DA_MERGE_ROWS = 256


def _attn_group(gi, d, n, sd_ref, x_ref, qd, kv, og, lg, bias_scr):
    B = DA_BLOCK
    L = DA_SPAN // d
    nblk = L // B
    nblk_log = int(math.log2(nblk))
    stride_r = L + B

    @pl.when(n == 0)
    def _():
        for r in range(d):
            kv[:, r * stride_r:r * stride_r + B, :] = jnp.zeros((2 * N_PAIR, B, LANES), bf16)

    scale = DA_HEAD_DIM ** -0.5
    for r in range(d):
        rows = pl.ds(r, L, stride=d) if d > 1 else pl.ds(0, L)
        for s in range(N_PAIR):
            qd[s, r * L:(r + 1) * L, :] = (x_ref[s, rows, :] * scale).astype(bf16)
        for s in range(2 * N_PAIR):
            kv[s, r * stride_r + B:r * stride_r + B + L, :] = x_ref[N_PAIR + s, rows, :].astype(bf16)

    qi = lax.broadcasted_iota(jnp.int32, (B, 2 * B), 0)
    kj = lax.broadcasted_iota(jnp.int32, (B, 2 * B), 1)
    rel = qi + B - kj
    valid = (rel >= 0) & (rel <= B)
    relf = rel.astype(f32)
    for h in range(DA_HEADS):
        bias = jnp.where(valid, -sd_ref[gi, h] * relf, NEG_INF)
        bias_scr[0, h] = bias
        bias_scr[1, h] = jnp.where(kj < B, NEG_INF, bias)

    lane = lax.broadcasted_iota(jnp.int32, (B, LANES), 1)
    lo_half = lane < DA_HEAD_DIM

    def blk_body(blk, carry):
        r = blk >> nblk_log
        m = blk & (nblk - 1)
        q0 = pl.multiple_of(blk * B, B)
        k0 = pl.multiple_of((blk + r) * B, B)
        first = jnp.logical_and(n == 0, m == 0).astype(jnp.int32)
        start = m * (B * d) + r
        orow = pl.ds(start, B, stride=d) if d > 1 else pl.ds(pl.multiple_of(start, B), B)
        for j in range(N_PAIR):
            qp = qd[j, pl.ds(q0, B), :]
            kp = kv[j, pl.ds(k0, 2 * B), :]
            vp = kv[N_PAIR + j, pl.ds(k0, 2 * B), :]
            outs, lses = [], []
            for e in range(2):
                keep = lo_half if e == 0 else jnp.logical_not(lo_half)
                qm = jnp.where(keep, qp, jnp.zeros_like(qp))
                s = _dot_nt(qm, kp) + bias_scr[first, 2 * j + e]
                mx = jnp.max(s, axis=-1, keepdims=True)
                ex = jnp.exp(s - mx)
                l = jnp.sum(ex, axis=-1, keepdims=True)
                outs.append(_dot(ex.astype(bf16), vp) * (1.0 / l))
                lses.append(jnp.broadcast_to(mx + jnp.log(l), (B, LANES)))
            og[gi, j, orow, :] = jnp.where(lo_half, outs[0], outs[1])
            lg[gi, j, orow, :] = jnp.where(lo_half, lses[0], lses[1])
        return carry

    lax.fori_loop(0, DA_SPAN // B, blk_body, 0, unroll=16)

    for r in range(d):
        kv[:, r * stride_r:r * stride_r + B, :] = kv[:, r * stride_r + L:r * stride_r + L + B, :]


def _attn_kernel(sd_ref, x_ref, o_ref, qd, kv0, kv1, kv2, og, lg, bias_scr):
    n = pl.program_id(1)
    g = pl.program_id(2)
    kvs = (kv0, kv1, kv2)
    for gi, (_, d) in enumerate(DA_PATTERNS):
        @pl.when(g == gi)
        def _(gi=gi, d=d):
            _attn_group(gi, d, n, sd_ref, x_ref, qd, kvs[gi], og, lg, bias_scr)

    @pl.when(g == len(DA_PATTERNS) - 1)
    def _():
        R = DA_MERGE_ROWS

        def merge(i, carry):
            r0 = pl.multiple_of(i * R, R)
            for j in range(N_PAIR):
                ls = [lg[gi, j, pl.ds(r0, R), :] for gi in range(len(DA_PATTERNS))]
                mx = functools.reduce(jnp.maximum, ls)
                ws = [jnp.exp(l - mx) for l in ls]
                den = functools.reduce(lambda a, b: a + b, ws)
                num = functools.reduce(lambda a, b: a + b,
                                       [w * og[gi, j, pl.ds(r0, R), :] for gi, w in enumerate(ws)])
                o_ref[pl.ds(r0, R), j * LANES:(j + 1) * LANES] = num * (1.0 / den)
            return carry

        lax.fori_loop(0, DA_SPAN // R, merge, 0)


def _attn_mixer(qkv_slabs, bsz, seq, sd):
    ng = len(DA_PATTERNS)
    n = qkv_slabs.shape[1]
    nt = seq // DA_SPAN
    per_g = 3 * N_PAIR
    kv_scratch = [pltpu.VMEM((2 * N_PAIR, DA_SPAN + DA_BLOCK * d, LANES), bf16) for _, d in DA_PATTERNS]
    return pl.pallas_call(
        _attn_kernel,
        grid=(bsz, nt, ng),
        in_specs=[pl.BlockSpec(memory_space=pltpu.SMEM),
                  pl.BlockSpec((per_g, DA_SPAN, LANES), lambda b, t, g: (g, b * nt + t, 0))],
        out_specs=pl.BlockSpec((DA_SPAN, DA_WIDTH), lambda b, t, g: (b * nt + t, 0)),
        out_shape=jax.ShapeDtypeStruct((n, DA_WIDTH), f32),
        scratch_shapes=[pltpu.VMEM((N_PAIR, DA_SPAN, LANES), bf16)] + kv_scratch + [
            pltpu.VMEM((ng, N_PAIR, DA_SPAN, LANES), f32),
            pltpu.VMEM((ng, N_PAIR, DA_SPAN, LANES), f32),
            pltpu.VMEM((2, DA_HEADS, DA_BLOCK, 2 * DA_BLOCK), f32)],
        compiler_params=_params(("arbitrary", "arbitrary", "arbitrary")),
        name="dilated_attn",
    )(sd, qkv_slabs)


POST_SUB = 256
POST_NSUB = 2
POST_TM = POST_SUB * POST_NSUB
FF_CHUNK = 256


def _post_kernel(ya_ref, yb_ref, h_ref, woa_ref, wob_ref, g_ref, wup_ref, cw_ref, cb_ref, wdn_ref,
                 o_ref, xn_scr, st_scr, carry_scr, acc_scr, act_scr):
    tm = POST_SUB
    C = FF_CHUNK
    dff = wdn_ref.shape[0]
    kc = cw_ref.shape[0]
    t = pl.program_id(1)
    nchunk = dff // C
    ns = C // LANES

    @pl.when(t == 0)
    def _():
        carry_scr[...] = jnp.zeros_like(carry_scr)

    def prologue(i):
        rows = pl.ds(i * tm, tm)
        y1 = (_dot(ya_ref[rows, :].astype(bf16), woa_ref[...])
              + _dot(yb_ref[rows, :].astype(bf16), wob_ref[...]))
        h1 = h_ref[rows, :] + _rms(y1, g_ref[1:2, :])
        o_ref[rows, :] = h1
        xn_scr[i] = _rms(h1, g_ref[2:3, :]).astype(bf16)
        acc_scr[i] = jnp.zeros(acc_scr.shape[1:], f32)

    def up(i, j):
        slot = j % 2
        xn = xn_scr[i]
        c0 = j * C
        rv = _dot(xn, wup_ref[:, c0:c0 + C])
        rg = _dot(xn, wup_ref[:, dff + c0:dff + c0 + C])
        st_scr[slot, :, 0:SUBLANES, :] = carry_scr[j]
        for s in range(ns):
            st_scr[slot, s, SUBLANES:SUBLANES + tm, :] = rv[:, s * LANES:(s + 1) * LANES]
            st_scr[slot, ns + s, SUBLANES:SUBLANES + tm, :] = rg[:, s * LANES:(s + 1) * LANES]

    def act(j):
        slot = j % 2
        parts = []
        for s in range(ns):
            c0 = j * C + s * LANES
            cv = cb_ref[:, c0:c0 + LANES]
            cg = cb_ref[:, dff + c0:dff + c0 + LANES]
            for k in range(kc):
                off = SUBLANES - (kc - 1) + k
                cv = cv + cw_ref[k:k + 1, c0:c0 + LANES] * st_scr[slot, s, off:off + tm, :]
                cg = cg + cw_ref[k:k + 1, dff + c0:dff + c0 + LANES] * st_scr[slot, ns + s, off:off + tm, :]
            parts.append((jax.nn.gelu(cg) * cv).astype(bf16))
        act_scr[slot] = jnp.concatenate(parts, axis=1)
        carry_scr[j] = st_scr[slot, :, tm:tm + SUBLANES, :]

    def down(i, j):
        acc_scr[i] += _dot(act_scr[j % 2], wdn_ref[j * C:(j + 1) * C, :])

    def ffn(i):
        up(i, 0)
        for j in range(nchunk):
            if j + 1 < nchunk:
                up(i, j + 1)
            if j >= 1:
                down(i, j - 1)
            act(j)
        down(i, nchunk - 1)

    def epilogue(i):
        rows = pl.ds(i * tm, tm)
        o_ref[rows, :] = o_ref[rows, :] + _rms(acc_scr[i], g_ref[3:4, :])

    for i in range(POST_NSUB):
        prologue(i)
    for i in range(POST_NSUB):
        ffn(i)
        epilogue(i)


def _layer_spec(shape, layer):
    nd = len(shape) - 1
    return pl.BlockSpec((None,) + tuple(shape[1:]), lambda *_: (layer,) + (0,) * nd,
                        pipeline_mode=pl.Buffered(1))


def _post(ya, yb, h2d, bsz, seq, w_out, layer, g_all, w_up_all, conv_w_all, conv_b_all, w_down_all):
    n, d = h2d.shape
    tm = POST_TM
    nt = seq // tm
    wa = ya.shape[1]
    wb = yb.shape[1]
    dff = w_down_all.shape[1]
    woa = w_out[:wa].astype(bf16)
    wob = w_out[wa:].astype(bf16)
    stacked = [g_all, w_up_all, conv_w_all, conv_b_all, w_down_all]
    consts = [woa, wob] + stacked
    row_spec = lambda w: pl.BlockSpec((tm, w), lambda b, t: (b * nt + t, 0))
    return pl.pallas_call(
        _post_kernel,
        grid=(bsz, nt),
        in_specs=[row_spec(wa), row_spec(wb), row_spec(d), _const_spec(woa.shape), _const_spec(wob.shape)]
                 + [_layer_spec(c.shape, layer) for c in stacked],
        out_specs=row_spec(d),
        out_shape=jax.ShapeDtypeStruct((n, d), f32),
        scratch_shapes=[pltpu.VMEM((POST_NSUB, POST_SUB, d), bf16),
                        pltpu.VMEM((2, 2 * FF_CHUNK // LANES, POST_SUB + SUBLANES, LANES), f32),
                        pltpu.VMEM((dff // FF_CHUNK, 2 * FF_CHUNK // LANES, SUBLANES, LANES), f32),
                        pltpu.VMEM((POST_NSUB, POST_SUB, d), f32),
                        pltpu.VMEM((2, POST_SUB, FF_CHUNK), bf16)],
        compiler_params=_params(("arbitrary", "arbitrary")),
        name="post_ffn",
    )(ya, yb, h2d, *consts)


def kernel(x, norm_g, ffn_w_up, ffn_conv_w, ffn_conv_b, ffn_w_down, ev_w_in, ev_w_out, rg_conv_w, rg_conv_b,
           rg_w_a, rg_b_a, rg_w_x, rg_b_x, rg_lambda, s5_a_re, s5_a_im, s5_b_re, s5_b_im, s5_c_re, s5_c_im,
           s5_d, s5_log_dt, s5_w_glu, s5_b_glu, od_w_in, od_w_out, hg_lower, hg_norm_g):
    bsz, seq, d = x.shape
    depth = norm_g.shape[0]
    assert seq % DA_SPAN == 0 and d % LANES == 0
    lb_p = jax.nn.softmax(hg_lower.astype(f32), axis=0)
    lb_all = jnp.cumsum(lb_p, axis=0) - lb_p[0]
    ngrp = len(DA_PATTERNS)
    slopes = (2.0 ** (-8.0 * jnp.arange(1, ngrp * DA_HEADS + 1, dtype=f32) / (ngrp * DA_HEADS))
              ).reshape(ngrp, DA_HEADS)
    sd = slopes * jnp.asarray([float(p[1]) for p in DA_PATTERNS], f32)[:, None]

    g_all = norm_g.astype(f32)
    w_up_all = ffn_w_up.astype(bf16)
    w_down_all = ffn_w_down.astype(bf16)
    conv_w_all = ffn_conv_w.astype(f32)
    conv_b_all = ffn_conv_b.astype(f32).reshape(depth, 1, -1)

    h = x.reshape(bsz * seq, d)
    for layer in range(depth):
        j = layer // 2
        g = norm_g[layer]
        if layer % 2 == 0:
            ya, yb = _even_mixer(h, g[0], ev_w_in[j], bsz, seq, rg_conv_w[j], rg_conv_b[j], rg_w_a[j], rg_b_a[j],
                                 rg_w_x[j], rg_b_x[j], rg_lambda[j], s5_a_re[j], s5_a_im[j],
                                 s5_b_re[j], s5_b_im[j], s5_c_re[j], s5_c_im[j], s5_d[j], s5_log_dt[j],
                                 s5_w_glu[j], s5_b_glu[j])
            w_out = ev_w_out[j]
        else:
            hgp, qkv = _norm_proj(h, g[0], od_w_in[j].astype(bf16),
                                  (("flat", 4 * HG_W), ("slab", ngrp * 3 * DA_WIDTH)))
            ya = _hgrn_mixer(hgp, bsz, seq, lb_all[layer], hg_norm_g[j])
            yb = _attn_mixer(qkv, bsz, seq, sd)
            w_out = od_w_out[j]
        h = _post(ya, yb, h, bsz, seq, w_out, layer, g_all, w_up_all, conv_w_all, conv_b_all, w_down_all)
    return h.reshape(bsz, seq, d)
```

```python
import functools
import math

import jax
import jax.numpy as jnp
from jax import lax
from jax.experimental import pallas as pl
from jax.experimental.pallas import tpu as pltpu

f32 = jnp.float32
bf16 = jnp.bfloat16

EPS = 1e-6
NEG_INF = -1e30
RG_C = 8.0
RG_HEADS = 8
RG_WIDTH = 512
S5_WIDTH = 256
S5_GROUP = 16
S5_STATE = 64
HG_HEADS = 4
HG_DK = 128
HG_CHUNK = 64
HG_W = HG_HEADS * HG_DK
DA_HEADS = 4
DA_HEAD_DIM = 64
DA_WIDTH = DA_HEADS * DA_HEAD_DIM
DA_PATTERNS = ((128, 1), (512, 4), (2048, 16))
DA_BLOCK = 128
DA_SPAN = 2048
LANES = 128
SUBLANES = 8
VMEM_LIMIT = 56 * 1024 * 1024


def _rms(x, g):
    return x * lax.rsqrt(jnp.mean(x * x, axis=-1, keepdims=True) + EPS) * g


def _dot(a, b):
    return jnp.dot(a, b, preferred_element_type=f32)


def _dot_nt(a, b):
    return lax.dot_general(a, b, (((1,), (1,)), ((), ())), preferred_element_type=f32)


def _dot_tn(a, b):
    return lax.dot_general(a, b, (((0,), (0,)), ((), ())), preferred_element_type=f32)


def _const_spec(shape):
    nd = len(shape)
    return pl.BlockSpec(shape, lambda *_: (0,) * nd, pipeline_mode=pl.Buffered(1))


def _params(sem):
    return pltpu.CompilerParams(dimension_semantics=sem, vmem_limit_bytes=VMEM_LIMIT)


PROJ_TM = 512
PROJ_CHUNK = 512


def _norm_proj_kernel(x_ref, g_ref, w_ref, *out_refs, layout):
    xn = _rms(x_ref[...], g_ref[...]).astype(bf16)
    col = 0
    for o_ref, (kind, width) in zip(out_refs, layout):
        for c0 in range(0, width, PROJ_CHUNK):
            cw = min(PROJ_CHUNK, width - c0)
            res = _dot(xn, w_ref[:, col + c0:col + c0 + cw])
            if kind == "flat":
                o_ref[:, c0:c0 + cw] = res
            else:
                for s in range(cw // LANES):
                    o_ref[c0 // LANES + s] = res[:, s * LANES:(s + 1) * LANES]
        col += width


def _norm_proj(h2d, g, w, layout):
    n, d = h2d.shape
    tm = PROJ_TM
    out_shape, out_specs = [], []
    for kind, width in layout:
        if kind == "flat":
            out_shape.append(jax.ShapeDtypeStruct((n, width), f32))
            out_specs.append(pl.BlockSpec((tm, width), lambda i: (i, 0)))
        else:
            out_shape.append(jax.ShapeDtypeStruct((width // LANES, n, LANES), f32))
            out_specs.append(pl.BlockSpec((width // LANES, tm, LANES), lambda i: (0, i, 0)))
    return pl.pallas_call(
        functools.partial(_norm_proj_kernel, layout=layout),
        grid=(n // tm,),
        in_specs=[pl.BlockSpec((tm, d), lambda i: (i, 0)),
                  _const_spec((1, d)),
                  _const_spec(w.shape)],
        out_specs=out_specs,
        out_shape=out_shape,
        compiler_params=_params(("arbitrary",)),
        name="norm_proj",
    )(h2d, g.reshape(1, d), w)


EV_T = 512
S5_L = 4
EV_PROJ_CHUNK = 256


def _even_kernel(h_ref, g_ref, win_ref, rcw_ref, rcb_ref, wa_ref, ba_ref, wx_ref, bx_ref, lam_ref,
                 ks_ref, bs_ref, ms_ref, lp_ref, pre_ref, pim_ref,
                 d5_ref, wg_ref, bg_ref,
                 ya_ref, yb_ref,
                 pj, xst, rga, rgb, rgc, ust, sre, sim, scr, sci, hst, yst):
    T = EV_T
    t = pl.program_id(1)

    @pl.when(t == 0)
    def _():
        xst[:, 0:SUBLANES, :] = jnp.zeros((RG_WIDTH // LANES, SUBLANES, LANES), f32)
        ust[:, 0:SUBLANES, :] = jnp.zeros((S5_WIDTH // LANES, SUBLANES, LANES), f32)
        rgc[...] = jnp.zeros_like(rgc)
        scr[...] = jnp.zeros_like(scr)
        sci[...] = jnp.zeros_like(sci)

    xn = _rms(h_ref[...], g_ref[...]).astype(bf16)
    for c0 in range(0, pj.shape[1], EV_PROJ_CHUNK):
        pj[:, c0:c0 + EV_PROJ_CHUNK] = _dot(xn, win_ref[:, c0:c0 + EV_PROJ_CHUNK])
    p_ref = pj

    kc = rcw_ref.shape[0]
    us = []
    for s in range(RG_WIDTH // LANES):
        cols = slice(s * LANES, (s + 1) * LANES)
        xst[s, SUBLANES:SUBLANES + T, :] = p_ref[:, cols]
        us_ = rcb_ref[:, cols]
        for k in range(kc):
            off = SUBLANES - (kc - 1) + k
            us_ = us_ + rcw_ref[k:k + 1, cols] * xst[s, off:off + T, :]
        us.append(us_)
    u = jnp.concatenate(us, axis=1)
    xst[:, 0:SUBLANES, :] = xst[:, T:T + SUBLANES, :]
    ub = u.astype(bf16)
    r = jax.nn.sigmoid(_dot(ub, wa_ref[...]) + ba_ref[...])
    ig = jax.nn.sigmoid(_dot(ub, wx_ref[...]) + bx_ref[...])
    log_a = (RG_C * r) * jax.nn.log_sigmoid(lam_ref[...])
    a = jnp.exp(log_a)
    rad = -jnp.tanh(log_a) * (a * a + 1.0)
    bt = jnp.where(rad > 0.0, rad * lax.rsqrt(rad), 0.0) * (ig * u)
    ngrp = T // SUBLANES
    a = a.reshape(ngrp, SUBLANES, RG_WIDTH)
    bt = bt.reshape(ngrp, SUBLANES, RG_WIDTH)
    rowm = lax.broadcasted_iota(jnp.int32, (1, SUBLANES, RG_WIDTH), 1)
    for s in (1, 2, 4):
        m = rowm >= s
        bt = bt + jnp.where(m, a * pltpu.roll(bt, s, 1), 0.0)
        a = a * jnp.where(m, pltpu.roll(a, s, 1), 1.0)
    rga[...] = a.reshape(T, RG_WIDTH)
    rgb[...] = bt.reshape(T, RG_WIDTH)
    ya_ref[...] = jax.nn.gelu(p_ref[:, RG_WIDTH:2 * RG_WIDTH])

    L = S5_L
    nc = T // L
    ns = sre.shape[1]
    nsl = S5_WIDTH // LANES
    u5 = p_ref[:, 2 * RG_WIDTH:2 * RG_WIDTH + S5_WIDTH]
    for s in range(nsl):
        ust[s, SUBLANES:SUBLANES + T, :] = u5[:, s * LANES:(s + 1) * LANES]
    rowl = lax.broadcasted_iota(jnp.int32, (T, LANES), 0) & (L - 1)
    pieces, ends = [], []
    for k in range(L):
        for s in range(nsl):
            p = ust[s, SUBLANES - k:SUBLANES - k + T, :]
            if k:
                p = jnp.where(rowl >= k, p, 0.0)
            pieces.append(p.astype(bf16))
            ends.append(ust[s, pl.ds(SUBLANES + L - 1 - k, nc, stride=L), :].astype(bf16))
    yb_ref[...] = _dot(jnp.concatenate(pieces, axis=1), ks_ref[...]) + d5_ref[...] * u5
    gend = _dot(jnp.concatenate(ends, axis=1), bs_ref[...])
    ngrp5 = nc // SUBLANES
    hre = gend[:, 0:ns].reshape(ngrp5, SUBLANES, ns)
    him = gend[:, ns:2 * ns].reshape(ngrp5, SUBLANES, ns)
    for i, s in enumerate((1, 2, 4)):
        cr = lp_ref[2 * i][None]
        ci = lp_ref[2 * i + 1][None]
        zr = pltpu.roll(hre, s, 1)
        zi = pltpu.roll(him, s, 1)
        hre, him = hre + (cr * zr - ci * zi), him + (cr * zi + ci * zr)
    sre[...] = hre.reshape(nc, ns)
    sim[...] = him.reshape(nc, ns)
    hst[0:SUBLANES, 0:ns] = scr[...]
    hst[0:SUBLANES, ns:2 * ns] = sci[...]

    def rg_body(gidx, c):
        r0 = gidx * SUBLANES
        h = rga[pl.ds(r0, SUBLANES), :] * c + rgb[pl.ds(r0, SUBLANES), :]
        rgb[pl.ds(r0, SUBLANES), :] = h
        return jnp.broadcast_to(h[SUBLANES - 1:SUBLANES, :], (SUBLANES, RG_WIDTH))

    c = rgc[...]
    for gidx in range(T // SUBLANES):
        c = rg_body(gidx, c)
    rgc[...] = c

    def s5_body(gidx, c):
        cre, cim = c
        r0 = gidx * SUBLANES
        pr = pre_ref[...]
        pi = pim_ref[...]
        hr = sre[pl.ds(r0, SUBLANES), :] + (pr * cre - pi * cim)
        hi = sim[pl.ds(r0, SUBLANES), :] + (pr * cim + pi * cre)
        sre[pl.ds(r0, SUBLANES), :] = hr
        sim[pl.ds(r0, SUBLANES), :] = hi
        return (jnp.broadcast_to(hr[SUBLANES - 1:SUBLANES, :], (SUBLANES, ns)),
                jnp.broadcast_to(hi[SUBLANES - 1:SUBLANES, :], (SUBLANES, ns)))

    cc = (scr[...], sci[...])
    for gidx in range(ngrp5):
        cc = s5_body(gidx, cc)
    cre, cim = cc
    scr[...] = cre
    sci[...] = cim
    hst[SUBLANES:SUBLANES + nc, 0:ns] = sre[...]
    hst[SUBLANES:SUBLANES + nc, ns:2 * ns] = sim[...]
    hprev = hst[SUBLANES - 1:SUBLANES - 1 + nc, :].astype(bf16)
    yi = _dot(hprev, ms_ref[...])
    for r in range(L):
        for s in range(nsl):
            c0 = r * S5_WIDTH + s * LANES
            yst[s, pl.ds(r, nc, stride=L), :] = yi[:, c0:c0 + LANES]
    ya_ref[...] = ya_ref[...] * rgb[...]
    y5 = yb_ref[...] + jnp.concatenate([yst[s] for s in range(nsl)], axis=1)
    v = jax.nn.gelu(y5)
    yb_ref[...] = v * jax.nn.sigmoid(_dot(v.astype(bf16), wg_ref[...]) + bg_ref[...])


def _block_diag(w):
    *lead, g, a, b = w.shape
    w2 = w.reshape(*lead, g * a, b).astype(bf16)
    rep = (jnp.arange(g * b)[None, :] % b == jnp.arange(b)[:, None]).astype(bf16)
    wide = jnp.einsum('...rb,bc->...rc', w2, rep, preferred_element_type=f32)
    own = (jnp.arange(g * a)[:, None] // a) == (jnp.arange(g * b)[None, :] // b)
    return jnp.where(own, wide, 0.0).astype(bf16)


def _even_mixer(h2d, g0, w_in, bsz, seq, rg_conv_w, rg_conv_b, rg_w_a, rg_b_a, rg_w_x, rg_b_x, rg_lambda,
                s5_a_re, s5_a_im, s5_b_re, s5_b_im, s5_c_re, s5_c_im, s5_d, s5_log_dt,
                s5_w_glu, s5_b_glu):
    n, dm = h2d.shape
    T = EV_T
    L = S5_L
    nt = seq // T
    ns = s5_a_re.shape[0] * S5_STATE
    dt = jnp.exp(s5_log_dt.astype(f32))[:, None]
    ar = s5_a_re.astype(f32)
    ai = s5_a_im.astype(f32)
    mag = jnp.exp(ar * dt)
    abar_re = mag * jnp.cos(ai * dt)
    abar_im = mag * jnp.sin(ai * dt)
    den = ar * ar + ai * ai
    num_re = abar_re - 1.0
    f_re = (num_re * ar + abar_im * ai) / den
    f_im = (abar_im * ar - num_re * ai) / den
    br = s5_b_re.astype(f32)
    bi = s5_b_im.astype(f32)
    bb_re = f_re[..., None] * br - f_im[..., None] * bi
    bb_im = f_re[..., None] * bi + f_im[..., None] * br
    cr = s5_c_re.astype(f32)
    ci = s5_c_im.astype(f32)
    jj = jnp.arange(SUBLANES * L + 1, dtype=f32)[:, None, None]
    pmag = jnp.exp(jj * (ar * dt)[None])
    pang = jj * (ai * dt)[None]
    pwr = pmag * jnp.cos(pang)
    pwi = pmag * jnp.sin(pang)
    lr, li = pwr[:L, :, None, :], pwi[:L, :, None, :]
    clr = cr[None] * lr - ci[None] * li
    cli = cr[None] * li + ci[None] * lr
    kk = jnp.einsum('kgpn,gnq->kgqp', clr, bb_re) - jnp.einsum('kgpn,gnq->kgqp', cli, bb_im)
    ks = _block_diag(kk).reshape(L * S5_WIDTH, S5_WIDTH).astype(bf16)
    lbr = pwr[:L, :, :, None] * bb_re[None] - pwi[:L, :, :, None] * bb_im[None]
    lbi = pwr[:L, :, :, None] * bb_im[None] + pwi[:L, :, :, None] * bb_re[None]
    bs = jnp.concatenate([_block_diag(jnp.swapaxes(lbr, 2, 3)), _block_diag(jnp.swapaxes(lbi, 2, 3))],
                         axis=2).reshape(L * S5_WIDTH, 2 * ns).astype(bf16)
    l1r, l1i = pwr[1:L + 1, :, None, :], pwi[1:L + 1, :, None, :]
    c1r = cr[None] * l1r - ci[None] * l1i
    c1i = cr[None] * l1i + ci[None] * l1r
    ms = jnp.concatenate([_block_diag(jnp.swapaxes(c1r, 2, 3)), _block_diag(-jnp.swapaxes(c1i, 2, 3))], axis=1)
    ms = jnp.swapaxes(ms, 0, 1).reshape(2 * ns, L * S5_WIDTH).astype(bf16)
    shifts = jnp.asarray((1, 2, 4))
    rows = jnp.arange(SUBLANES)[None, :, None]
    keep = rows >= shifts[:, None, None]
    lp = jnp.stack([jnp.where(keep, p[L * shifts].reshape(3, 1, ns), 0.0) for p in (pwr, pwi)],
                   axis=1).reshape(6, SUBLANES, ns)
    p_re = pwr[L::L].reshape(SUBLANES, ns)
    p_im = pwi[L::L].reshape(SUBLANES, ns)
    wa_bd = _block_diag(rg_w_a).astype(bf16)
    wx_bd = _block_diag(rg_w_x).astype(bf16)
    row = lambda v: v.reshape(1, -1).astype(f32)
    consts = [row(g0), w_in.astype(bf16),
              rg_conv_w.astype(f32), row(rg_conv_b), wa_bd, row(rg_b_a), wx_bd, row(rg_b_x), row(rg_lambda),
              ks, bs, ms, lp, p_re, p_im, row(s5_d),
              s5_w_glu.astype(bf16), row(s5_b_glu)]
    width = w_in.shape[1]
    nc = T // L
    return pl.pallas_call(
        _even_kernel,
        grid=(bsz, nt),
        in_specs=[pl.BlockSpec((T, dm), lambda b, t: (b * nt + t, 0))]
                 + [_const_spec(c.shape) for c in consts],
        out_specs=[pl.BlockSpec((T, RG_WIDTH), lambda b, t: (b * nt + t, 0)),
                   pl.BlockSpec((T, S5_WIDTH), lambda b, t: (b * nt + t, 0))],
        out_shape=[jax.ShapeDtypeStruct((n, RG_WIDTH), f32),
                   jax.ShapeDtypeStruct((n, S5_WIDTH), f32)],
        scratch_shapes=[pltpu.VMEM((T, width), f32),
                        pltpu.VMEM((RG_WIDTH // LANES, T + SUBLANES, LANES), f32),
                        pltpu.VMEM((T, RG_WIDTH), f32), pltpu.VMEM((T, RG_WIDTH), f32),
                        pltpu.VMEM((SUBLANES, RG_WIDTH), f32),
                        pltpu.VMEM((S5_WIDTH // LANES, T + SUBLANES, LANES), f32),
                        pltpu.VMEM((nc, ns), f32), pltpu.VMEM((nc, ns), f32),
                        pltpu.VMEM((SUBLANES, ns), f32), pltpu.VMEM((SUBLANES, ns), f32),
                        pltpu.VMEM((SUBLANES + nc, 2 * ns), f32),
                        pltpu.VMEM((S5_WIDTH // LANES, T, LANES), f32)],
        compiler_params=_params(("arbitrary", "arbitrary")),
        name="even_mixer",
    )(h2d, *consts)


HG_T = 512
HG_GROUP = 4
HG_UNROLL = 2


def _split3(x):
    hi = x.astype(bf16)
    r1 = x - hi.astype(f32)
    mid = r1.astype(bf16)
    lo = (r1 - mid.astype(f32)).astype(bf16)
    return hi, mid, lo


def _hgrn_kernel(p_ref, lb_ref, nw_ref, o_ref, st_ref):
    C = HG_CHUNK
    t = pl.program_id(1)

    @pl.when(t == 0)
    def _():
        st_ref[...] = jnp.zeros_like(st_ref)

    ri = lax.broadcasted_iota(jnp.int32, (C, C), 0)
    ci = lax.broadcasted_iota(jnp.int32, (C, C), 1)
    causal = ri >= ci
    tril = causal.astype(bf16)
    lb = lb_ref[...]

    heads = range(HG_HEADS)
    sls = [slice(h * HG_DK, (h + 1) * HG_DK) for h in heads]

    def group(gidx, carry):
        prep = []
        for ci_ in range(HG_GROUP):
            r0 = pl.multiple_of((gidx * HG_GROUP + ci_) * C, C)
            q = p_ref[pl.ds(r0, C), 0:HG_W]
            z = p_ref[pl.ds(r0, C), HG_W:2 * HG_W]
            sg = jax.nn.sigmoid(z)
            log_f = jnp.log(lb + (1.0 - lb) * sg)
            kk = (1.0 - lb) * (1.0 - sg)
            hi, mid_, lo = _split3(log_f)
            cum = _dot(tril, hi) + _dot(tril, mid_) + _dot(tril, lo)
            prep.append((r0, jax.nn.silu(q), kk, cum))
        mats = []
        for r0, qs, kk, cum in prep:
            midv = cum[C // 2:C // 2 + 1, :]
            last = cum[C - 1:C, :]
            qa32 = qs * jnp.exp(cum - midv)
            kb32 = kk * jnp.exp(midv - cum)
            qa = qa32.astype(bf16)
            kb = kb32.astype(bf16)
            qi = (qa32 * jnp.exp(midv)).astype(bf16)
            kc = (kb32 * jnp.exp(last - midv)).astype(bf16)
            v = [p_ref[pl.ds(r0, C), 2 * HG_W + h * HG_DK:2 * HG_W + (h + 1) * HG_DK].astype(bf16)
                 for h in heads]
            sc = [_dot_nt(qa[:, sl], kb[:, sl]) for sl in sls]
            con_t = [_dot_tn(v[h], kc[:, sl]) for h, sl in enumerate(sls)]
            mats.append((r0, qi, v, sc, con_t, jnp.exp(last)))
        st = [st_ref[h] for h in heads]
        outs = []
        for r0, qi, v, sc, con_t, dec in mats:
            oi = [_dot_nt(qi[:, sl], st[h].astype(bf16)) for h, sl in enumerate(sls)]
            st = [dec[:, sl] * st[h] + con_t[h] for h, sl in enumerate(sls)]
            outs.append((r0, v, sc, oi))
        for h in heads:
            st_ref[h] = st[h]
        for r0, v, sc, oi in outs:
            o = [_dot(jnp.where(causal, sc[h], 0.0).astype(bf16), v[h]) + oi[h] for h in heads]
            for h, sl in enumerate(sls):
                g_h = p_ref[pl.ds(r0, C), 3 * HG_W + h * HG_DK:3 * HG_W + (h + 1) * HG_DK]
                on = o[h] * lax.rsqrt(jnp.mean(o[h] * o[h], axis=-1, keepdims=True) + EPS) * nw_ref[...]
                o_ref[pl.ds(r0, C), sl] = on * jax.nn.silu(g_h)
        return carry

    lax.fori_loop(0, HG_T // (C * HG_GROUP), group, 0, unroll=HG_UNROLL)


def _hgrn_mixer(hg, bsz, seq, lb, norm_w):
    n = hg.shape[0]
    T = HG_T
    nt = seq // T
    return pl.pallas_call(
        _hgrn_kernel,
        grid=(bsz, nt),
        in_specs=[pl.BlockSpec((T, 4 * HG_W), lambda b, t: (b * nt + t, 0)),
                  _const_spec((1, HG_W)), _const_spec((1, HG_DK))],
        out_specs=pl.BlockSpec((T, HG_W), lambda b, t: (b * nt + t, 0)),
        out_shape=jax.ShapeDtypeStruct((n, HG_W), f32),
        scratch_shapes=[pltpu.VMEM((HG_HEADS, HG_DK, HG_DK), f32)],
        compiler_params=_params(("arbitrary", "arbitrary")),
        name="hgrn_mixer",
    )(hg, lb.reshape(1, HG_W).astype(f32), norm_w.reshape(1, HG_DK).astype(f32))


N_PAIR = DA_WIDTH // LANES
DA_MERGE_ROWS = 256


def _attn_group(gi, d, n, sd_ref, x_ref, qd, kv, og, lg, bias_scr):
    B = DA_BLOCK
    L = DA_SPAN // d
    nblk = L // B
    nblk_log = int(math.log2(nblk))
    stride_r = L + B

    @pl.when(n == 0)
    def _():
        for r in range(d):
            kv[:, r * stride_r:r * stride_r + B, :] = jnp.zeros((2 * N_PAIR, B, LANES), bf16)

    scale = DA_HEAD_DIM ** -0.5
    for r in range(d):
        rows = pl.ds(r, L, stride=d) if d > 1 else pl.ds(0, L)
        for s in range(N_PAIR):
            qd[s, r * L:(r + 1) * L, :] = (x_ref[s, rows, :] * scale).astype(bf16)
        for s in range(2 * N_PAIR):
            kv[s, r * stride_r + B:r * stride_r + B + L, :] = x_ref[N_PAIR + s, rows, :].astype(bf16)

    qi = lax.broadcasted_iota(jnp.int32, (B, 2 * B), 0)
    kj = lax.broadcasted_iota(jnp.int32, (B, 2 * B), 1)
    rel = qi + B - kj
    valid = (rel >= 0) & (rel <= B)
    relf = rel.astype(f32)
    for h in range(DA_HEADS):
        bias = jnp.where(valid, -sd_ref[gi, h] * relf, NEG_INF)
        bias_scr[0, h] = bias
        bias_scr[1, h] = jnp.where(kj < B, NEG_INF, bias)

    lane = lax.broadcasted_iota(jnp.int32, (B, LANES), 1)
    lo_half = lane < DA_HEAD_DIM

    def blk_body(blk, carry):
        r = blk >> nblk_log
        m = blk & (nblk - 1)
        q0 = pl.multiple_of(blk * B, B)
        k0 = pl.multiple_of((blk + r) * B, B)
        first = jnp.logical_and(n == 0, m == 0).astype(jnp.int32)
        start = m * (B * d) + r
        orow = pl.ds(start, B, stride=d) if d > 1 else pl.ds(pl.multiple_of(start, B), B)
        for j in range(N_PAIR):
            qp = qd[j, pl.ds(q0, B), :]
            kp = kv[j, pl.ds(k0, 2 * B), :]
            vp = kv[N_PAIR + j, pl.ds(k0, 2 * B), :]
            outs, lses = [], []
            for e in range(2):
                keep = lo_half if e == 0 else jnp.logical_not(lo_half)
                qm = jnp.where(keep, qp, jnp.zeros_like(qp))
                s = _dot_nt(qm, kp) + bias_scr[first, 2 * j + e]
                mx = jnp.max(s, axis=-1, keepdims=True)
                ex = jnp.exp(s - mx)
                l = jnp.sum(ex, axis=-1, keepdims=True)
                outs.append(_dot(ex.astype(bf16), vp) * (1.0 / l))
                lses.append(jnp.broadcast_to(mx + jnp.log(l), (B, LANES)))
            og[gi, j, orow, :] = jnp.where(lo_half, outs[0], outs[1])
            lg[gi, j, orow, :] = jnp.where(lo_half, lses[0], lses[1])
        return carry

    lax.fori_loop(0, DA_SPAN // B, blk_body, 0, unroll=16)

    for r in range(d):
        kv[:, r * stride_r:r * stride_r + B, :] = kv[:, r * stride_r + L:r * stride_r + L + B, :]


def _attn_kernel(sd_ref, x_ref, o_ref, qd, kv0, kv1, kv2, og, lg, bias_scr):
    n = pl.program_id(1)
    g = pl.program_id(2)
    kvs = (kv0, kv1, kv2)
    for gi, (_, d) in enumerate(DA_PATTERNS):
        @pl.when(g == gi)
        def _(gi=gi, d=d):
            _attn_group(gi, d, n, sd_ref, x_ref, qd, kvs[gi], og, lg, bias_scr)

    @pl.when(g == len(DA_PATTERNS) - 1)
    def _():
        R = DA_MERGE_ROWS

        def merge(i, carry):
            r0 = pl.multiple_of(i * R, R)
            for j in range(N_PAIR):
                ls = [lg[gi, j, pl.ds(r0, R), :] for gi in range(len(DA_PATTERNS))]
                mx = functools.reduce(jnp.maximum, ls)
                ws = [jnp.exp(l - mx) for l in ls]
                den = functools.reduce(lambda a, b: a + b, ws)
                num = functools.reduce(lambda a, b: a + b,
                                       [w * og[gi, j, pl.ds(r0, R), :] for gi, w in enumerate(ws)])
                o_ref[pl.ds(r0, R), j * LANES:(j + 1) * LANES] = num * (1.0 / den)
            return carry

        lax.fori_loop(0, DA_SPAN // R, merge, 0)


def _attn_mixer(qkv_slabs, bsz, seq, sd):
    ng = len(DA_PATTERNS)
    n = qkv_slabs.shape[1]
    nt = seq // DA_SPAN
    per_g = 3 * N_PAIR
    kv_scratch = [pltpu.VMEM((2 * N_PAIR, DA_SPAN + DA_BLOCK * d, LANES), bf16) for _, d in DA_PATTERNS]
    return pl.pallas_call(
        _attn_kernel,
        grid=(bsz, nt, ng),
        in_specs=[pl.BlockSpec(memory_space=pltpu.SMEM),
                  pl.BlockSpec((per_g, DA_SPAN, LANES), lambda b, t, g: (g, b * nt + t, 0))],
        out_specs=pl.BlockSpec((DA_SPAN, DA_WIDTH), lambda b, t, g: (b * nt + t, 0)),
        out_shape=jax.ShapeDtypeStruct((n, DA_WIDTH), f32),
        scratch_shapes=[pltpu.VMEM((N_PAIR, DA_SPAN, LANES), bf16)] + kv_scratch + [
            pltpu.VMEM((ng, N_PAIR, DA_SPAN, LANES), f32),
            pltpu.VMEM((ng, N_PAIR, DA_SPAN, LANES), f32),
            pltpu.VMEM((2, DA_HEADS, DA_BLOCK, 2 * DA_BLOCK), f32)],
        compiler_params=_params(("arbitrary", "arbitrary", "arbitrary")),
        name="dilated_attn",
    )(sd, qkv_slabs)


POST_SUB = 256
POST_NSUB = 2
POST_TM = POST_SUB * POST_NSUB
FF_CHUNK = 256


def _post_kernel(ya_ref, yb_ref, h_ref, woa_ref, wob_ref, g_ref, wup_ref, cw_ref, cb_ref, wdn_ref,
                 o_ref, xn_scr, st_scr, carry_scr, acc_scr, act_scr):
    tm = POST_SUB
    C = FF_CHUNK
    dff = wdn_ref.shape[0]
    kc = cw_ref.shape[0]
    t = pl.program_id(1)
    nchunk = dff // C
    ns = C // LANES

    @pl.when(t == 0)
    def _():
        carry_scr[...] = jnp.zeros_like(carry_scr)

    def prologue(i):
        rows = pl.ds(i * tm, tm)
        y1 = (_dot(ya_ref[rows, :].astype(bf16), woa_ref[...])
              + _dot(yb_ref[rows, :].astype(bf16), wob_ref[...]))
        h1 = h_ref[rows, :] + _rms(y1, g_ref[1:2, :])
        o_ref[rows, :] = h1
        xn_scr[i] = _rms(h1, g_ref[2:3, :]).astype(bf16)
        acc_scr[i] = jnp.zeros(acc_scr.shape[1:], f32)

    def up(i, j):
        slot = j % 2
        xn = xn_scr[i]
        c0 = j * C
        rv = _dot(xn, wup_ref[:, c0:c0 + C])
        rg = _dot(xn, wup_ref[:, dff + c0:dff + c0 + C])
        st_scr[slot, :, 0:SUBLANES, :] = carry_scr[j]
        for s in range(ns):
            st_scr[slot, s, SUBLANES:SUBLANES + tm, :] = rv[:, s * LANES:(s + 1) * LANES]
            st_scr[slot, ns + s, SUBLANES:SUBLANES + tm, :] = rg[:, s * LANES:(s + 1) * LANES]

    def act(j):
        slot = j % 2
        parts = []
        for s in range(ns):
            c0 = j * C + s * LANES
            cv = cb_ref[:, c0:c0 + LANES]
            cg = cb_ref[:, dff + c0:dff + c0 + LANES]
            for k in range(kc):
                off = SUBLANES - (kc - 1) + k
                cv = cv + cw_ref[k:k + 1, c0:c0 + LANES] * st_scr[slot, s, off:off + tm, :]
                cg = cg + cw_ref[k:k + 1, dff + c0:dff + c0 + LANES] * st_scr[slot, ns + s, off:off + tm, :]
            parts.append((jax.nn.gelu(cg) * cv).astype(bf16))
        act_scr[slot] = jnp.concatenate(parts, axis=1)
        carry_scr[j] = st_scr[slot, :, tm:tm + SUBLANES, :]

    def down(i, j):
        acc_scr[i] += _dot(act_scr[j % 2], wdn_ref[j * C:(j + 1) * C, :])

    def ffn(i):
        up(i, 0)
        for j in range(nchunk):
            if j + 1 < nchunk:
                up(i, j + 1)
            if j >= 1:
                down(i, j - 1)
            act(j)
        down(i, nchunk - 1)

    def epilogue(i):
        rows = pl.ds(i * tm, tm)
        o_ref[rows, :] = o_ref[rows, :] + _rms(acc_scr[i], g_ref[3:4, :])

    for i in range(POST_NSUB):
        prologue(i)
    for i in range(POST_NSUB):
        ffn(i)
        epilogue(i)


def _layer_spec(shape, layer):
    nd = len(shape) - 1
    return pl.BlockSpec((None,) + tuple(shape[1:]), lambda *_: (layer,) + (0,) * nd,
                        pipeline_mode=pl.Buffered(1))


def _post(ya, yb, h2d, bsz, seq, w_out, layer, g_all, w_up_all, conv_w_all, conv_b_all, w_down_all):
    n, d = h2d.shape
    tm = POST_TM
    nt = seq // tm
    wa = ya.shape[1]
    wb = yb.shape[1]
    dff = w_down_all.shape[1]
    woa = w_out[:wa].astype(bf16)
    wob = w_out[wa:].astype(bf16)
    stacked = [g_all, w_up_all, conv_w_all, conv_b_all, w_down_all]
    consts = [woa, wob] + stacked
    row_spec = lambda w: pl.BlockSpec((tm, w), lambda b, t: (b * nt + t, 0))
    return pl.pallas_call(
        _post_kernel,
        grid=(bsz, nt),
        in_specs=[row_spec(wa), row_spec(wb), row_spec(d), _const_spec(woa.shape), _const_spec(wob.shape)]
                 + [_layer_spec(c.shape, layer) for c in stacked],
        out_specs=row_spec(d),
        out_shape=jax.ShapeDtypeStruct((n, d), f32),
        scratch_shapes=[pltpu.VMEM((POST_NSUB, POST_SUB, d), bf16),
                        pltpu.VMEM((2, 2 * FF_CHUNK // LANES, POST_SUB + SUBLANES, LANES), f32),
                        pltpu.VMEM((dff // FF_CHUNK, 2 * FF_CHUNK // LANES, SUBLANES, LANES), f32),
                        pltpu.VMEM((POST_NSUB, POST_SUB, d), f32),
                        pltpu.VMEM((2, POST_SUB, FF_CHUNK), bf16)],
        compiler_params=_params(("arbitrary", "arbitrary")),
        name="post_ffn",
    )(ya, yb, h2d, *consts)


def kernel(x, norm_g, ffn_w_up, ffn_conv_w, ffn_conv_b, ffn_w_down, ev_w_in, ev_w_out, rg_conv_w, rg_conv_b,
           rg_w_a, rg_b_a, rg_w_x, rg_b_x, rg_lambda, s5_a_re, s5_a_im, s5_b_re, s5_b_im, s5_c_re, s5_c_im,
           s5_d, s5_log_dt, s5_w_glu, s5_b_glu, od_w_in, od_w_out, hg_lower, hg_norm_g):
    bsz, seq, d = x.shape
    depth = norm_g.shape[0]
    assert seq % DA_SPAN == 0 and d % LANES == 0
    lb_p = jax.nn.softmax(hg_lower.astype(f32), axis=0)
    lb_all = jnp.cumsum(lb_p, axis=0) - lb_p[0]
    ngrp = len(DA_PATTERNS)
    slopes = (2.0 ** (-8.0 * jnp.arange(1, ngrp * DA_HEADS + 1, dtype=f32) / (ngrp * DA_HEADS))
              ).reshape(ngrp, DA_HEADS)
    sd = slopes * jnp.asarray([float(p[1]) for p in DA_PATTERNS], f32)[:, None]

    g_all = norm_g.astype(f32)
    w_up_all = ffn_w_up.astype(bf16)
    w_down_all = ffn_w_down.astype(bf16)
    conv_w_all = ffn_conv_w.astype(f32)
    conv_b_all = ffn_conv_b.astype(f32).reshape(depth, 1, -1)

    h = x.reshape(bsz * seq, d)
    for layer in range(depth):
        j = layer // 2
        g = norm_g[layer]
        if layer % 2 == 0:
            ya, yb = _even_mixer(h, g[0], ev_w_in[j], bsz, seq, rg_conv_w[j], rg_conv_b[j], rg_w_a[j], rg_b_a[j],
                                 rg_w_x[j], rg_b_x[j], rg_lambda[j], s5_a_re[j], s5_a_im[j],
                                 s5_b_re[j], s5_b_im[j], s5_c_re[j], s5_c_im[j], s5_d[j], s5_log_dt[j],
                                 s5_w_glu[j], s5_b_glu[j])
            w_out = ev_w_out[j]
        else:
            hgp, qkv = _norm_proj(h, g[0], od_w_in[j].astype(bf16),
                                  (("flat", 4 * HG_W), ("slab", ngrp * 3 * DA_WIDTH)))
            ya = _hgrn_mixer(hgp, bsz, seq, lb_all[layer], hg_norm_g[j])
            yb = _attn_mixer(qkv, bsz, seq, sd)
            w_out = od_w_out[j]
        h = _post(ya, yb, h, bsz, seq, w_out, layer, g_all, w_up_all, conv_w_all, conv_b_all, w_down_all)
    return h.reshape(bsz, seq, d)
```

```python
import functools
import math

import jax
import jax.numpy as jnp
from jax import lax
from jax.experimental import pallas as pl
from jax.experimental.pallas import tpu as pltpu

f32 = jnp.float32
bf16 = jnp.bfloat16

EPS = 1e-6
NEG_INF = -1e30
RG_C = 8.0
RG_HEADS = 8
RG_WIDTH = 512
S5_WIDTH = 256
S5_GROUP = 16
S5_STATE = 64
HG_HEADS = 4
HG_DK = 128
HG_CHUNK = 64
HG_W = HG_HEADS * HG_DK
DA_HEADS = 4
DA_HEAD_DIM = 64
DA_WIDTH = DA_HEADS * DA_HEAD_DIM
DA_PATTERNS = ((128, 1), (512, 4), (2048, 16))
DA_BLOCK = 128
DA_SPAN = 2048
LANES = 128
SUBLANES = 8
VMEM_LIMIT = 56 * 1024 * 1024


def _rms(x, g):
    return x * lax.rsqrt(jnp.mean(x * x, axis=-1, keepdims=True) + EPS) * g


def _dot(a, b):
    return jnp.dot(a, b, preferred_element_type=f32)


def _dot_nt(a, b):
    return lax.dot_general(a, b, (((1,), (1,)), ((), ())), preferred_element_type=f32)


def _dot_tn(a, b):
    return lax.dot_general(a, b, (((0,), (0,)), ((), ())), preferred_element_type=f32)


def _const_spec(shape):
    nd = len(shape)
    return pl.BlockSpec(shape, lambda *_: (0,) * nd, pipeline_mode=pl.Buffered(1))


def _params(sem, fuse_inputs=None):
    return pltpu.CompilerParams(dimension_semantics=sem, vmem_limit_bytes=VMEM_LIMIT,
                                allow_input_fusion=fuse_inputs)


PROJ_TM = 512
PROJ_CHUNK = 512


def _norm_proj_kernel(x_ref, g_ref, w_ref, *out_refs, layout):
    xn = _rms(x_ref[...], g_ref[...]).astype(bf16)
    col = 0
    for o_ref, (kind, width) in zip(out_refs, layout):
        for c0 in range(0, width, PROJ_CHUNK):
            cw = min(PROJ_CHUNK, width - c0)
            res = _dot(xn, w_ref[:, col + c0:col + c0 + cw])
            if kind == "flat":
                o_ref[:, c0:c0 + cw] = res
            else:
                for s in range(cw // LANES):
                    o_ref[c0 // LANES + s] = res[:, s * LANES:(s + 1) * LANES]
        col += width


def _norm_proj(h2d, g, w, layout):
    n, d = h2d.shape
    tm = PROJ_TM
    out_shape, out_specs = [], []
    for kind, width in layout:
        if kind == "flat":
            out_shape.append(jax.ShapeDtypeStruct((n, width), f32))
            out_specs.append(pl.BlockSpec((tm, width), lambda i: (i, 0)))
        else:
            out_shape.append(jax.ShapeDtypeStruct((width // LANES, n, LANES), f32))
            out_specs.append(pl.BlockSpec((width // LANES, tm, LANES), lambda i: (0, i, 0)))
    return pl.pallas_call(
        functools.partial(_norm_proj_kernel, layout=layout),
        grid=(n // tm,),
        in_specs=[pl.BlockSpec((tm, d), lambda i: (i, 0)),
                  _const_spec((1, d)),
                  _const_spec(w.shape)],
        out_specs=out_specs,
        out_shape=out_shape,
        compiler_params=_params(("arbitrary",)),
        name="norm_proj",
    )(h2d, g.reshape(1, d), w)


EV_T = 512
S5_L = 4
EV_PROJ_CHUNK = 256


def _even_kernel(h_ref, g_ref, win_ref, rcw_ref, rcb_ref, wa_ref, ba_ref, wx_ref, bx_ref, lam_ref,
                 ks_ref, bs_ref, ms_ref, lp_ref, pre_ref, pim_ref,
                 d5_ref, wg_ref, bg_ref,
                 ya_ref, yb_ref,
                 pj, xst, rga, rgb, rgc, ust, sre, sim, scr, sci, hst, yst):
    T = EV_T
    t = pl.program_id(1)

    @pl.when(t == 0)
    def _():
        xst[:, 0:SUBLANES, :] = jnp.zeros((RG_WIDTH // LANES, SUBLANES, LANES), f32)
        ust[:, 0:SUBLANES, :] = jnp.zeros((S5_WIDTH // LANES, SUBLANES, LANES), f32)
        rgc[...] = jnp.zeros_like(rgc)
        scr[...] = jnp.zeros_like(scr)
        sci[...] = jnp.zeros_like(sci)

    xn = _rms(h_ref[...], g_ref[...]).astype(bf16)
    for c0 in range(0, pj.shape[1], EV_PROJ_CHUNK):
        pj[:, c0:c0 + EV_PROJ_CHUNK] = _dot(xn, win_ref[:, c0:c0 + EV_PROJ_CHUNK])
    p_ref = pj

    kc = rcw_ref.shape[0]
    us = []
    for s in range(RG_WIDTH // LANES):
        cols = slice(s * LANES, (s + 1) * LANES)
        xst[s, SUBLANES:SUBLANES + T, :] = p_ref[:, cols]
        us_ = rcb_ref[:, cols]
        for k in range(kc):
            off = SUBLANES - (kc - 1) + k
            us_ = us_ + rcw_ref[k:k + 1, cols] * xst[s, off:off + T, :]
        us.append(us_)
    u = jnp.concatenate(us, axis=1)
    xst[:, 0:SUBLANES, :] = xst[:, T:T + SUBLANES, :]
    ub = u.astype(bf16)
    r = jax.nn.sigmoid(_dot(ub, wa_ref[...]) + ba_ref[...])
    ig = jax.nn.sigmoid(_dot(ub, wx_ref[...]) + bx_ref[...])
    log_a = (RG_C * r) * jax.nn.log_sigmoid(lam_ref[...])
    a = jnp.exp(log_a)
    rad = -jnp.tanh(log_a) * (a * a + 1.0)
    bt = jnp.where(rad > 0.0, rad * lax.rsqrt(rad), 0.0) * (ig * u)
    ngrp = T // SUBLANES
    a = a.reshape(ngrp, SUBLANES, RG_WIDTH)
    bt = bt.reshape(ngrp, SUBLANES, RG_WIDTH)
    rowm = lax.broadcasted_iota(jnp.int32, (1, SUBLANES, RG_WIDTH), 1)
    for s in (1, 2, 4):
        m = rowm >= s
        bt = bt + jnp.where(m, a * pltpu.roll(bt, s, 1), 0.0)
        a = a * jnp.where(m, pltpu.roll(a, s, 1), 1.0)
    rga[...] = a.reshape(T, RG_WIDTH)
    rgb[...] = bt.reshape(T, RG_WIDTH)
    ya_ref[...] = jax.nn.gelu(p_ref[:, RG_WIDTH:2 * RG_WIDTH])

    L = S5_L
    nc = T // L
    ns = sre.shape[1]
    nsl = S5_WIDTH // LANES
    u5 = p_ref[:, 2 * RG_WIDTH:2 * RG_WIDTH + S5_WIDTH]
    for s in range(nsl):
        ust[s, SUBLANES:SUBLANES + T, :] = u5[:, s * LANES:(s + 1) * LANES]
    rowl = lax.broadcasted_iota(jnp.int32, (T, LANES), 0) & (L - 1)
    pieces, ends = [], []
    for k in range(L):
        for s in range(nsl):
            p = ust[s, SUBLANES - k:SUBLANES - k + T, :]
            if k:
                p = jnp.where(rowl >= k, p, 0.0)
            pieces.append(p.astype(bf16))
            ends.append(ust[s, pl.ds(SUBLANES + L - 1 - k, nc, stride=L), :].astype(bf16))
    yb_ref[...] = _dot(jnp.concatenate(pieces, axis=1), ks_ref[...]) + d5_ref[...] * u5
    gend = _dot(jnp.concatenate(ends, axis=1), bs_ref[...])
    ngrp5 = nc // SUBLANES
    hre = gend[:, 0:ns].reshape(ngrp5, SUBLANES, ns)
    him = gend[:, ns:2 * ns].reshape(ngrp5, SUBLANES, ns)
    for i, s in enumerate((1, 2, 4)):
        cr = lp_ref[2 * i][None]
        ci = lp_ref[2 * i + 1][None]
        zr = pltpu.roll(hre, s, 1)
        zi = pltpu.roll(him, s, 1)
        hre, him = hre + (cr * zr - ci * zi), him + (cr * zi + ci * zr)
    sre[...] = hre.reshape(nc, ns)
    sim[...] = him.reshape(nc, ns)
    hst[0:SUBLANES, 0:ns] = scr[...]
    hst[0:SUBLANES, ns:2 * ns] = sci[...]

    def rg_body(gidx, c):
        r0 = gidx * SUBLANES
        h = rga[pl.ds(r0, SUBLANES), :] * c + rgb[pl.ds(r0, SUBLANES), :]
        rgb[pl.ds(r0, SUBLANES), :] = h
        return jnp.broadcast_to(h[SUBLANES - 1:SUBLANES, :], (SUBLANES, RG_WIDTH))

    c = rgc[...]
    for gidx in range(T // SUBLANES):
        c = rg_body(gidx, c)
    rgc[...] = c

    def s5_body(gidx, c):
        cre, cim = c
        r0 = gidx * SUBLANES
        pr = pre_ref[...]
        pi = pim_ref[...]
        hr = sre[pl.ds(r0, SUBLANES), :] + (pr * cre - pi * cim)
        hi = sim[pl.ds(r0, SUBLANES), :] + (pr * cim + pi * cre)
        sre[pl.ds(r0, SUBLANES), :] = hr
        sim[pl.ds(r0, SUBLANES), :] = hi
        return (jnp.broadcast_to(hr[SUBLANES - 1:SUBLANES, :], (SUBLANES, ns)),
                jnp.broadcast_to(hi[SUBLANES - 1:SUBLANES, :], (SUBLANES, ns)))

    cc = (scr[...], sci[...])
    for gidx in range(ngrp5):
        cc = s5_body(gidx, cc)
    cre, cim = cc
    scr[...] = cre
    sci[...] = cim
    hst[SUBLANES:SUBLANES + nc, 0:ns] = sre[...]
    hst[SUBLANES:SUBLANES + nc, ns:2 * ns] = sim[...]
    hprev = hst[SUBLANES - 1:SUBLANES - 1 + nc, :].astype(bf16)
    yi = _dot(hprev, ms_ref[...])
    for r in range(L):
        for s in range(nsl):
            c0 = r * S5_WIDTH + s * LANES
            yst[s, pl.ds(r, nc, stride=L), :] = yi[:, c0:c0 + LANES]
    ya_ref[...] = ya_ref[...] * rgb[...]
    y5 = yb_ref[...] + jnp.concatenate([yst[s] for s in range(nsl)], axis=1)
    v = jax.nn.gelu(y5)
    yb_ref[...] = v * jax.nn.sigmoid(_dot(v.astype(bf16), wg_ref[...]) + bg_ref[...])


def _block_diag(w):
    *lead, g, a, b = w.shape
    w2 = w.reshape(*lead, g * a, b).astype(bf16)
    rep = (jnp.arange(g * b)[None, :] % b == jnp.arange(b)[:, None]).astype(bf16)
    wide = jnp.einsum('...rb,bc->...rc', w2, rep, preferred_element_type=f32)
    own = (jnp.arange(g * a)[:, None] // a) == (jnp.arange(g * b)[None, :] // b)
    return jnp.where(own, wide, 0.0).astype(bf16)


def _even_mixer(h2d, g0, w_in, bsz, seq, rg_conv_w, rg_conv_b, rg_w_a, rg_b_a, rg_w_x, rg_b_x, rg_lambda,
                s5_a_re, s5_a_im, s5_b_re, s5_b_im, s5_c_re, s5_c_im, s5_d, s5_log_dt,
                s5_w_glu, s5_b_glu):
    n, dm = h2d.shape
    T = EV_T
    L = S5_L
    nt = seq // T
    ns = s5_a_re.shape[0] * S5_STATE
    dt = jnp.exp(s5_log_dt.astype(f32))[:, None]
    ar = s5_a_re.astype(f32)
    ai = s5_a_im.astype(f32)
    mag = jnp.exp(ar * dt)
    abar_re = mag * jnp.cos(ai * dt)
    abar_im = mag * jnp.sin(ai * dt)
    den = ar * ar + ai * ai
    num_re = abar_re - 1.0
    f_re = (num_re * ar + abar_im * ai) / den
    f_im = (abar_im * ar - num_re * ai) / den
    br = s5_b_re.astype(f32)
    bi = s5_b_im.astype(f32)
    bb_re = f_re[..., None] * br - f_im[..., None] * bi
    bb_im = f_re[..., None] * bi + f_im[..., None] * br
    cr = s5_c_re.astype(f32)
    ci = s5_c_im.astype(f32)
    jj = jnp.arange(SUBLANES * L + 1, dtype=f32)[:, None, None]
    pmag = jnp.exp(jj * (ar * dt)[None])
    pang = jj * (ai * dt)[None]
    pwr = pmag * jnp.cos(pang)
    pwi = pmag * jnp.sin(pang)
    lr, li = pwr[:L, :, None, :], pwi[:L, :, None, :]
    clr = cr[None] * lr - ci[None] * li
    cli = cr[None] * li + ci[None] * lr
    kk = jnp.einsum('kgpn,gnq->kgqp', clr, bb_re) - jnp.einsum('kgpn,gnq->kgqp', cli, bb_im)
    ks = _block_diag(kk).reshape(L * S5_WIDTH, S5_WIDTH).astype(bf16)
    lbr = pwr[:L, :, :, None] * bb_re[None] - pwi[:L, :, :, None] * bb_im[None]
    lbi = pwr[:L, :, :, None] * bb_im[None] + pwi[:L, :, :, None] * bb_re[None]
    bs = jnp.concatenate([_block_diag(jnp.swapaxes(lbr, 2, 3)), _block_diag(jnp.swapaxes(lbi, 2, 3))],
                         axis=2).reshape(L * S5_WIDTH, 2 * ns).astype(bf16)
    l1r, l1i = pwr[1:L + 1, :, None, :], pwi[1:L + 1, :, None, :]
    c1r = cr[None] * l1r - ci[None] * l1i
    c1i = cr[None] * l1i + ci[None] * l1r
    ms = jnp.concatenate([_block_diag(jnp.swapaxes(c1r, 2, 3)), _block_diag(-jnp.swapaxes(c1i, 2, 3))], axis=1)
    ms = jnp.swapaxes(ms, 0, 1).reshape(2 * ns, L * S5_WIDTH).astype(bf16)
    shifts = jnp.asarray((1, 2, 4))
    rows = jnp.arange(SUBLANES)[None, :, None]
    keep = rows >= shifts[:, None, None]
    lp = jnp.stack([jnp.where(keep, p[L * shifts].reshape(3, 1, ns), 0.0) for p in (pwr, pwi)],
                   axis=1).reshape(6, SUBLANES, ns)
    p_re = pwr[L::L].reshape(SUBLANES, ns)
    p_im = pwi[L::L].reshape(SUBLANES, ns)
    wa_bd = _block_diag(rg_w_a).astype(bf16)
    wx_bd = _block_diag(rg_w_x).astype(bf16)
    row = lambda v: v.reshape(1, -1).astype(f32)
    consts = [row(g0), w_in.astype(bf16),
              rg_conv_w.astype(f32), row(rg_conv_b), wa_bd, row(rg_b_a), wx_bd, row(rg_b_x), row(rg_lambda),
              ks, bs, ms, lp, p_re, p_im, row(s5_d),
              s5_w_glu.astype(bf16), row(s5_b_glu)]
    width = w_in.shape[1]
    nc = T // L
    return pl.pallas_call(
        _even_kernel,
        grid=(bsz, nt),
        in_specs=[pl.BlockSpec((T, dm), lambda b, t: (b * nt + t, 0))]
                 + [_const_spec(c.shape) for c in consts],
        out_specs=[pl.BlockSpec((T, RG_WIDTH), lambda b, t: (b * nt + t, 0)),
                   pl.BlockSpec((T, S5_WIDTH), lambda b, t: (b * nt + t, 0))],
        out_shape=[jax.ShapeDtypeStruct((n, RG_WIDTH), f32),
                   jax.ShapeDtypeStruct((n, S5_WIDTH), f32)],
        scratch_shapes=[pltpu.VMEM((T, width), f32),
                        pltpu.VMEM((RG_WIDTH // LANES, T + SUBLANES, LANES), f32),
                        pltpu.VMEM((T, RG_WIDTH), f32), pltpu.VMEM((T, RG_WIDTH), f32),
                        pltpu.VMEM((SUBLANES, RG_WIDTH), f32),
                        pltpu.VMEM((S5_WIDTH // LANES, T + SUBLANES, LANES), f32),
                        pltpu.VMEM((nc, ns), f32), pltpu.VMEM((nc, ns), f32),
                        pltpu.VMEM((SUBLANES, ns), f32), pltpu.VMEM((SUBLANES, ns), f32),
                        pltpu.VMEM((SUBLANES + nc, 2 * ns), f32),
                        pltpu.VMEM((S5_WIDTH // LANES, T, LANES), f32)],
        compiler_params=_params(("arbitrary", "arbitrary")),
        name="even_mixer",
    )(h2d, *consts)


HG_T = 512
HG_GROUP = 4
HG_UNROLL = 2


def _split3(x):
    hi = x.astype(bf16)
    r1 = x - hi.astype(f32)
    mid = r1.astype(bf16)
    lo = (r1 - mid.astype(f32)).astype(bf16)
    return hi, mid, lo


def _hgrn_kernel(p_ref, lb_ref, nw_ref, o_ref, st_ref):
    C = HG_CHUNK
    t = pl.program_id(1)

    @pl.when(t == 0)
    def _():
        st_ref[...] = jnp.zeros_like(st_ref)

    ri = lax.broadcasted_iota(jnp.int32, (C, C), 0)
    ci = lax.broadcasted_iota(jnp.int32, (C, C), 1)
    causal = ri >= ci
    tril = causal.astype(bf16)
    lb = lb_ref[...]

    heads = range(HG_HEADS)
    sls = [slice(h * HG_DK, (h + 1) * HG_DK) for h in heads]

    def group(gidx, carry):
        prep = []
        for ci_ in range(HG_GROUP):
            r0 = pl.multiple_of((gidx * HG_GROUP + ci_) * C, C)
            q = p_ref[pl.ds(r0, C), 0:HG_W]
            z = p_ref[pl.ds(r0, C), HG_W:2 * HG_W]
            sg = jax.nn.sigmoid(z)
            log_f = jnp.log(lb + (1.0 - lb) * sg)
            kk = (1.0 - lb) * (1.0 - sg)
            hi, mid_, lo = _split3(log_f)
            cum = _dot(tril, hi) + _dot(tril, mid_) + _dot(tril, lo)
            prep.append((r0, jax.nn.silu(q), kk, cum))
        mats = []
        for r0, qs, kk, cum in prep:
            midv = cum[C // 2:C // 2 + 1, :]
            last = cum[C - 1:C, :]
            qa32 = qs * jnp.exp(cum - midv)
            kb32 = kk * jnp.exp(midv - cum)
            qa = qa32.astype(bf16)
            kb = kb32.astype(bf16)
            qi = (qa32 * jnp.exp(midv)).astype(bf16)
            kc = (kb32 * jnp.exp(last - midv)).astype(bf16)
            v = [p_ref[pl.ds(r0, C), 2 * HG_W + h * HG_DK:2 * HG_W + (h + 1) * HG_DK].astype(bf16)
                 for h in heads]
            sc = [_dot_nt(qa[:, sl], kb[:, sl]) for sl in sls]
            con_t = [_dot_tn(v[h], kc[:, sl]) for h, sl in enumerate(sls)]
            mats.append((r0, qi, v, sc, con_t, jnp.exp(last)))
        st = [st_ref[h] for h in heads]
        outs = []
        for r0, qi, v, sc, con_t, dec in mats:
            oi = [_dot_nt(qi[:, sl], st[h].astype(bf16)) for h, sl in enumerate(sls)]
            st = [dec[:, sl] * st[h] + con_t[h] for h, sl in enumerate(sls)]
            outs.append((r0, v, sc, oi))
        for h in heads:
            st_ref[h] = st[h]
        for r0, v, sc, oi in outs:
            o = [_dot(jnp.where(causal, sc[h], 0.0).astype(bf16), v[h]) + oi[h] for h in heads]
            for h, sl in enumerate(sls):
                g_h = p_ref[pl.ds(r0, C), 3 * HG_W + h * HG_DK:3 * HG_W + (h + 1) * HG_DK]
                on = o[h] * lax.rsqrt(jnp.mean(o[h] * o[h], axis=-1, keepdims=True) + EPS) * nw_ref[...]
                o_ref[pl.ds(r0, C), sl] = on * jax.nn.silu(g_h)
        return carry

    lax.fori_loop(0, HG_T // (C * HG_GROUP), group, 0, unroll=HG_UNROLL)


def _hgrn_mixer(hg, bsz, seq, lb, norm_w):
    n = hg.shape[0]
    T = HG_T
    nt = seq // T
    return pl.pallas_call(
        _hgrn_kernel,
        grid=(bsz, nt),
        in_specs=[pl.BlockSpec((T, 4 * HG_W), lambda b, t: (b * nt + t, 0)),
                  _const_spec((1, HG_W)), _const_spec((1, HG_DK))],
        out_specs=pl.BlockSpec((T, HG_W), lambda b, t: (b * nt + t, 0)),
        out_shape=jax.ShapeDtypeStruct((n, HG_W), f32),
        scratch_shapes=[pltpu.VMEM((HG_HEADS, HG_DK, HG_DK), f32)],
        compiler_params=_params(("arbitrary", "arbitrary")),
        name="hgrn_mixer",
    )(hg, lb.reshape(1, HG_W).astype(f32), norm_w.reshape(1, HG_DK).astype(f32))


N_PAIR = DA_WIDTH // LANES
DA_MERGE_ROWS = 256


def _attn_group(gi, d, n, sd_ref, x_ref, qd, kv, og, lg, bias_scr):
    B = DA_BLOCK
    L = DA_SPAN // d
    nblk = L // B
    nblk_log = int(math.log2(nblk))
    stride_r = L + B

    @pl.when(n == 0)
    def _():
        for r in range(d):
            kv[:, r * stride_r:r * stride_r + B, :] = jnp.zeros((2 * N_PAIR, B, LANES), bf16)

    scale = DA_HEAD_DIM ** -0.5
    for r in range(d):
        rows = pl.ds(r, L, stride=d) if d > 1 else pl.ds(0, L)
        for s in range(N_PAIR):
            qd[s, r * L:(r + 1) * L, :] = (x_ref[s, rows, :] * scale).astype(bf16)
        for s in range(2 * N_PAIR):
            kv[s, r * stride_r + B:r * stride_r + B + L, :] = x_ref[N_PAIR + s, rows, :].astype(bf16)

    qi = lax.broadcasted_iota(jnp.int32, (B, 2 * B), 0)
    kj = lax.broadcasted_iota(jnp.int32, (B, 2 * B), 1)
    rel = qi + B - kj
    valid = (rel >= 0) & (rel <= B)
    relf = rel.astype(f32)
    for h in range(DA_HEADS):
        bias = jnp.where(valid, -sd_ref[gi, h] * relf, NEG_INF)
        bias_scr[0, h] = bias
        bias_scr[1, h] = jnp.where(kj < B, NEG_INF, bias)

    lane = lax.broadcasted_iota(jnp.int32, (B, LANES), 1)
    lo_half = lane < DA_HEAD_DIM

    def blk_body(blk, carry):
        r = blk >> nblk_log
        m = blk & (nblk - 1)
        q0 = pl.multiple_of(blk * B, B)
        k0 = pl.multiple_of((blk + r) * B, B)
        first = jnp.logical_and(n == 0, m == 0).astype(jnp.int32)
        start = m * (B * d) + r
        orow = pl.ds(start, B, stride=d) if d > 1 else pl.ds(pl.multiple_of(start, B), B)
        for j in range(N_PAIR):
            qp = qd[j, pl.ds(q0, B), :]
            kp = kv[j, pl.ds(k0, 2 * B), :]
            vp = kv[N_PAIR + j, pl.ds(k0, 2 * B), :]
            outs, lses = [], []
            for e in range(2):
                keep = lo_half if e == 0 else jnp.logical_not(lo_half)
                qm = jnp.where(keep, qp, jnp.zeros_like(qp))
                s = _dot_nt(qm, kp) + bias_scr[first, 2 * j + e]
                mx = jnp.max(s, axis=-1, keepdims=True)
                ex = jnp.exp(s - mx)
                l = jnp.sum(ex, axis=-1, keepdims=True)
                outs.append(_dot(ex.astype(bf16), vp) * (1.0 / l))
                lses.append(jnp.broadcast_to(mx + jnp.log(l), (B, LANES)))
            og[gi, j, orow, :] = jnp.where(lo_half, outs[0], outs[1])
            lg[gi, j, orow, :] = jnp.where(lo_half, lses[0], lses[1])
        return carry

    lax.fori_loop(0, DA_SPAN // B, blk_body, 0, unroll=16)

    for r in range(d):
        kv[:, r * stride_r:r * stride_r + B, :] = kv[:, r * stride_r + L:r * stride_r + L + B, :]


def _attn_kernel(sd_ref, x_ref, o_ref, qd, kv0, kv1, kv2, og, lg, bias_scr):
    n = pl.program_id(1)
    g = pl.program_id(2)
    kvs = (kv0, kv1, kv2)
    for gi, (_, d) in enumerate(DA_PATTERNS):
        @pl.when(g == gi)
        def _(gi=gi, d=d):
            _attn_group(gi, d, n, sd_ref, x_ref, qd, kvs[gi], og, lg, bias_scr)

    @pl.when(g == len(DA_PATTERNS) - 1)
    def _():
        R = DA_MERGE_ROWS

        def merge(i, carry):
            r0 = pl.multiple_of(i * R, R)
            for j in range(N_PAIR):
                ls = [lg[gi, j, pl.ds(r0, R), :] for gi in range(len(DA_PATTERNS))]
                mx = functools.reduce(jnp.maximum, ls)
                ws = [jnp.exp(l - mx) for l in ls]
                den = functools.reduce(lambda a, b: a + b, ws)
                num = functools.reduce(lambda a, b: a + b,
                                       [w * og[gi, j, pl.ds(r0, R), :] for gi, w in enumerate(ws)])
                o_ref[pl.ds(r0, R), j * LANES:(j + 1) * LANES] = num * (1.0 / den)
            return carry

        lax.fori_loop(0, DA_SPAN // R, merge, 0)


def _attn_mixer(qkv_slabs, bsz, seq, sd):
    ng = len(DA_PATTERNS)
    n = qkv_slabs.shape[1]
    nt = seq // DA_SPAN
    per_g = 3 * N_PAIR
    kv_scratch = [pltpu.VMEM((2 * N_PAIR, DA_SPAN + DA_BLOCK * d, LANES), bf16) for _, d in DA_PATTERNS]
    return pl.pallas_call(
        _attn_kernel,
        grid=(bsz, nt, ng),
        in_specs=[pl.BlockSpec(memory_space=pltpu.SMEM),
                  pl.BlockSpec((per_g, DA_SPAN, LANES), lambda b, t, g: (g, b * nt + t, 0))],
        out_specs=pl.BlockSpec((DA_SPAN, DA_WIDTH), lambda b, t, g: (b * nt + t, 0)),
        out_shape=jax.ShapeDtypeStruct((n, DA_WIDTH), f32),
        scratch_shapes=[pltpu.VMEM((N_PAIR, DA_SPAN, LANES), bf16)] + kv_scratch + [
            pltpu.VMEM((ng, N_PAIR, DA_SPAN, LANES), f32),
            pltpu.VMEM((ng, N_PAIR, DA_SPAN, LANES), f32),
            pltpu.VMEM((2, DA_HEADS, DA_BLOCK, 2 * DA_BLOCK), f32)],
        compiler_params=_params(("arbitrary", "arbitrary", "arbitrary")),
        name="dilated_attn",
    )(sd, qkv_slabs)


POST_SUB = 256
POST_NSUB = 2
POST_TM = POST_SUB * POST_NSUB
FF_CHUNK = 256


def _post_kernel(ya_ref, yb_ref, h_ref, woa_ref, wob_ref, g_ref, wup_ref, cw_ref, cb_ref, wdn_ref,
                 o_ref, xn_scr, st_scr, carry_scr, acc_scr, act_scr):
    tm = POST_SUB
    C = FF_CHUNK
    dff = wdn_ref.shape[0]
    kc = cw_ref.shape[0]
    t = pl.program_id(1)
    nchunk = dff // C
    ns = C // LANES

    @pl.when(t == 0)
    def _():
        carry_scr[...] = jnp.zeros_like(carry_scr)

    def prologue(i):
        rows = pl.ds(i * tm, tm)
        y1 = (_dot(ya_ref[rows, :].astype(bf16), woa_ref[...])
              + _dot(yb_ref[rows, :].astype(bf16), wob_ref[...]))
        h1 = h_ref[rows, :] + _rms(y1, g_ref[1:2, :])
        o_ref[rows, :] = h1
        xn_scr[i] = _rms(h1, g_ref[2:3, :]).astype(bf16)
        acc_scr[i] = jnp.zeros(acc_scr.shape[1:], f32)

    def up(i, j):
        slot = j % 2
        xn = xn_scr[i]
        c0 = j * C
        rv = _dot(xn, wup_ref[:, c0:c0 + C])
        rg = _dot(xn, wup_ref[:, dff + c0:dff + c0 + C])
        st_scr[slot, :, 0:SUBLANES, :] = carry_scr[j]
        for s in range(ns):
            st_scr[slot, s, SUBLANES:SUBLANES + tm, :] = rv[:, s * LANES:(s + 1) * LANES]
            st_scr[slot, ns + s, SUBLANES:SUBLANES + tm, :] = rg[:, s * LANES:(s + 1) * LANES]

    def act(j):
        slot = j % 2
        parts = []
        for s in range(ns):
            c0 = j * C + s * LANES
            cv = cb_ref[:, c0:c0 + LANES]
            cg = cb_ref[:, dff + c0:dff + c0 + LANES]
            for k in range(kc):
                off = SUBLANES - (kc - 1) + k
                cv = cv + cw_ref[k:k + 1, c0:c0 + LANES] * st_scr[slot, s, off:off + tm, :]
                cg = cg + cw_ref[k:k + 1, dff + c0:dff + c0 + LANES] * st_scr[slot, ns + s, off:off + tm, :]
            parts.append((jax.nn.gelu(cg) * cv).astype(bf16))
        act_scr[slot] = jnp.concatenate(parts, axis=1)
        carry_scr[j] = st_scr[slot, :, tm:tm + SUBLANES, :]

    def down(i, j):
        acc_scr[i] += _dot(act_scr[j % 2], wdn_ref[j * C:(j + 1) * C, :])

    def ffn(i):
        up(i, 0)
        for j in range(nchunk):
            if j + 1 < nchunk:
                up(i, j + 1)
            if j >= 1:
                down(i, j - 1)
            act(j)
        down(i, nchunk - 1)

    def epilogue(i):
        rows = pl.ds(i * tm, tm)
        o_ref[rows, :] = o_ref[rows, :] + _rms(acc_scr[i], g_ref[3:4, :])

    for i in range(POST_NSUB):
        prologue(i)
    for i in range(POST_NSUB):
        ffn(i)
        epilogue(i)


def _layer_spec(shape, layer):
    nd = len(shape) - 1
    return pl.BlockSpec((None,) + tuple(shape[1:]), lambda *_: (layer,) + (0,) * nd,
                        pipeline_mode=pl.Buffered(1))


def _post(ya, yb, h2d, bsz, seq, w_out, layer, g_all, w_up_all, conv_w_all, conv_b_all, w_down_all):
    n, d = h2d.shape
    tm = POST_TM
    nt = seq // tm
    wa = ya.shape[1]
    wb = yb.shape[1]
    dff = w_down_all.shape[1]
    woa = w_out[:wa].astype(bf16)
    wob = w_out[wa:].astype(bf16)
    stacked = [g_all, w_up_all, conv_w_all, conv_b_all, w_down_all]
    consts = [woa, wob] + stacked
    row_spec = lambda w: pl.BlockSpec((tm, w), lambda b, t: (b * nt + t, 0))
    return pl.pallas_call(
        _post_kernel,
        grid=(bsz, nt),
        in_specs=[row_spec(wa), row_spec(wb), row_spec(d), _const_spec(woa.shape), _const_spec(wob.shape)]
                 + [_layer_spec(c.shape, layer) for c in stacked],
        out_specs=row_spec(d),
        out_shape=jax.ShapeDtypeStruct((n, d), f32),
        scratch_shapes=[pltpu.VMEM((POST_NSUB, POST_SUB, d), bf16),
                        pltpu.VMEM((2, 2 * FF_CHUNK // LANES, POST_SUB + SUBLANES, LANES), f32),
                        pltpu.VMEM((dff // FF_CHUNK, 2 * FF_CHUNK // LANES, SUBLANES, LANES), f32),
                        pltpu.VMEM((POST_NSUB, POST_SUB, d), f32),
                        pltpu.VMEM((2, POST_SUB, FF_CHUNK), bf16)],
        compiler_params=_params(("arbitrary", "arbitrary"),
                                fuse_inputs=[False] * 6 + [True, False, False, True]),
        name="post_ffn",
    )(ya, yb, h2d, *consts)


def kernel(x, norm_g, ffn_w_up, ffn_conv_w, ffn_conv_b, ffn_w_down, ev_w_in, ev_w_out, rg_conv_w, rg_conv_b,
           rg_w_a, rg_b_a, rg_w_x, rg_b_x, rg_lambda, s5_a_re, s5_a_im, s5_b_re, s5_b_im, s5_c_re, s5_c_im,
           s5_d, s5_log_dt, s5_w_glu, s5_b_glu, od_w_in, od_w_out, hg_lower, hg_norm_g):
    bsz, seq, d = x.shape
    depth = norm_g.shape[0]
    assert seq % DA_SPAN == 0 and d % LANES == 0
    lb_p = jax.nn.softmax(hg_lower.astype(f32), axis=0)
    lb_all = jnp.cumsum(lb_p, axis=0) - lb_p[0]
    ngrp = len(DA_PATTERNS)
    slopes = (2.0 ** (-8.0 * jnp.arange(1, ngrp * DA_HEADS + 1, dtype=f32) / (ngrp * DA_HEADS))
              ).reshape(ngrp, DA_HEADS)
    sd = slopes * jnp.asarray([float(p[1]) for p in DA_PATTERNS], f32)[:, None]

    g_all = norm_g.astype(f32)
    w_up_all = ffn_w_up.astype(bf16)
    w_down_all = ffn_w_down.astype(bf16)
    conv_w_all = ffn_conv_w.astype(f32)
    conv_b_all = ffn_conv_b.astype(f32).reshape(depth, 1, -1)

    h = x.reshape(bsz * seq, d)
    for layer in range(depth):
        j = layer // 2
        g = norm_g[layer]
        if layer % 2 == 0:
            ya, yb = _even_mixer(h, g[0], ev_w_in[j], bsz, seq, rg_conv_w[j], rg_conv_b[j], rg_w_a[j], rg_b_a[j],
                                 rg_w_x[j], rg_b_x[j], rg_lambda[j], s5_a_re[j], s5_a_im[j],
                                 s5_b_re[j], s5_b_im[j], s5_c_re[j], s5_c_im[j], s5_d[j], s5_log_dt[j],
                                 s5_w_glu[j], s5_b_glu[j])
            w_out = ev_w_out[j]
        else:
            hgp, qkv = _norm_proj(h, g[0], od_w_in[j].astype(bf16),
                                  (("flat", 4 * HG_W), ("slab", ngrp * 3 * DA_WIDTH)))
            ya = _hgrn_mixer(hgp, bsz, seq, lb_all[layer], hg_norm_g[j])
            yb = _attn_mixer(qkv, bsz, seq, sd)
            w_out = od_w_out[j]
        h = _post(ya, yb, h, bsz, seq, w_out, layer, g_all, w_up_all, conv_w_all, conv_b_all, w_down_all)
    return h.reshape(bsz, seq, d)
```
